```python
import math
import jax, jax.numpy as jnp
from jax import lax
import numpy as np

D_MODEL = 1024
BATCH = 8
SEQ = 4096
DEPTH = 1
DEC_BATCH = 32
DEC_SEQ = 16
PAST_LEN = 1024

CHUNK = 64
Q_BLOCK = 128
FOX_HEADS = 8
FOX_HEAD_DIM = D_MODEL // 16
FOX_W = FOX_HEADS * FOX_HEAD_DIM
FOX_SCALE = FOX_HEAD_DIM ** -0.5
GDN_HEADS = 4
GDN_DK = D_MODEL // 8
GDN_DV = D_MODEL // 8
GDN_KW = GDN_HEADS * GDN_DK
GDN_VW = GDN_HEADS * GDN_DV
GDN_CONV_CH = 2 * GDN_KW + GDN_VW
CONV_W = 4
N_EXPERTS = 32
TOP_K = 4
D_FF = D_MODEL
SWIGLU_LIMIT = 7.0
SWIGLU_ALPHA = 1.702
MOE_BLOCK = 128
FORGET_BIAS_INIT = 2.0
RMS_EPS = 1e-6
NEG_INF = -1e30
IN_SIZES = (FOX_W, FOX_W, FOX_W, FOX_HEADS, GDN_CONV_CH, GDN_VW, GDN_HEADS, GDN_HEADS, D_MODEL, D_MODEL)
IN_W = 3 * FOX_W + FOX_HEADS + GDN_CONV_CH + GDN_VW + 2 * GDN_HEADS + 2 * D_MODEL

kernel_name = 'hybrid_fox_gdn_moe_stream'


def _rms_norm(x, w):
    xf = x.astype(jnp.float32)
    xf = xf * lax.rsqrt(jnp.mean(xf * xf, axis=-1, keepdims=True) + RMS_EPS)
    return (xf * w.astype(jnp.float32)).astype(x.dtype)


def _l2_normalize(x):
    xf = x.astype(jnp.float32)
    return xf * lax.rsqrt(jnp.sum(xf * xf, axis=-1, keepdims=True) + RMS_EPS)


def _split_columns(p):
    outs, start = [], 0
    for size in IN_SIZES:
        outs.append(p[..., start:start + size])
        start += size
    return outs


def _fox_prompt(q, k, v, logf):
    B, S, H, Dh = q.shape
    nb = S // Q_BLOCK
    ck = jnp.cumsum(logf, axis=1).transpose(0, 2, 1)
    qb = q.reshape(B, nb, Q_BLOCK, H, Dh).transpose(1, 0, 2, 3, 4)
    cqb = ck.reshape(B, H, nb, Q_BLOCK).transpose(2, 0, 1, 3)
    k_pos = jnp.arange(S)

    def one_block(args):
        i, qi, cqi = args
        q_pos = i * Q_BLOCK + jnp.arange(Q_BLOCK)
        s = jnp.einsum('bqhd,bkhd->bhqk', qi, k).astype(jnp.float32) * FOX_SCALE
        s = s + (cqi[..., :, None] - ck[:, :, None, :])
        s = jnp.where(q_pos[:, None] >= k_pos[None, :], s, NEG_INF)
        p = jax.nn.softmax(s, axis=-1)
        return jnp.einsum('bhqk,bkhd->bqhd', p.astype(v.dtype), v)

    o = lax.map(one_block, (jnp.arange(nb), qb, cqb))
    return o.transpose(1, 0, 2, 3, 4).reshape(B, S, H, Dh)


def _fox_step(q, k, v, logf, k_cache, v_cache, logf_cache):
    B, T, H, Dh = q.shape
    P = k_cache.shape[1]
    k_all = jnp.concatenate([k_cache.astype(k.dtype), k], axis=1)
    v_all = jnp.concatenate([v_cache.astype(v.dtype), v], axis=1)
    c = jnp.cumsum(jnp.concatenate([logf_cache.astype(jnp.float32), logf], axis=1), axis=1)
    c = c.transpose(0, 2, 1)
    cq = c[:, :, P:]
    s = jnp.einsum('bqhd,bkhd->bhqk', q, k_all).astype(jnp.float32) * FOX_SCALE
    s = s + (cq[..., :, None] - c[:, :, None, :])
    mask = jnp.arange(P + T)[None, :] <= (P + jnp.arange(T))[:, None]
    s = jnp.where(mask, s, NEG_INF)
    p = jax.nn.softmax(s, axis=-1)
    return jnp.einsum('bhqk,bkhd->bqhd', p.astype(v_all.dtype), v_all)


def _gdn_block(S, xs):
    q, k, v, g, beta = xs
    L = q.shape[-2]
    G = jnp.cumsum(g, axis=-1)
    tri_incl = jnp.tril(jnp.ones((L, L), dtype=bool))
    tri_strict = jnp.tril(jnp.ones((L, L), dtype=bool), -1)
    decay = jnp.exp(jnp.where(tri_incl, G[..., :, None] - G[..., None, :], -jnp.inf))
    kk = jnp.einsum('bhid,bhjd->bhij', k, k)
    A = jnp.where(tri_strict, kk * decay, 0.0) * beta[..., :, None]
    eG = jnp.exp(G)[..., None]
    rhs = beta[..., None] * (v - eG * jnp.einsum('bhld,bhdv->bhlv', k, S))
    delta = lax.linalg.triangular_solve(jnp.eye(L, dtype=jnp.float32) + A, rhs,
                                        left_side=True, lower=True, unit_diagonal=True)
    qk = jnp.einsum('bhid,bhjd->bhij', q, k) * decay
    o = eG * jnp.einsum('bhld,bhdv->bhlv', q, S) + jnp.einsum('bhij,bhjv->bhiv', qk, delta)
    G_last = G[..., -1:]
    S_new = jnp.exp(G_last)[..., None] * S + jnp.einsum('bhld,bhlv->bhdv', k * jnp.exp(G_last - G)[..., None], delta)
    return S_new, o


def _gated_delta(qkv, z, a, b, conv_buf, S0, conv_w, a_log, dt_bias, norm_w):
    B, T, _ = qkv.shape
    xpad = jnp.concatenate([conv_buf.astype(qkv.dtype), qkv], axis=1)
    conv = xpad[:, 0:T] * conv_w[0]
    for i in range(1, CONV_W):
        conv = conv + xpad[:, i:i + T] * conv_w[i]
    new_buf = xpad[:, T:]
    conv = jax.nn.silu(conv)
    q = _l2_normalize(conv[..., :GDN_KW].reshape(B, T, GDN_HEADS, GDN_DK)) * (GDN_DK ** -0.5)
    k = _l2_normalize(conv[..., GDN_KW:2 * GDN_KW].reshape(B, T, GDN_HEADS, GDN_DK))
    v = conv[..., 2 * GDN_KW:].reshape(B, T, GDN_HEADS, GDN_DV).astype(jnp.float32)
    g = -jnp.exp(a_log.astype(jnp.float32)) * jax.nn.softplus((a + dt_bias).astype(jnp.float32))
    beta = jax.nn.sigmoid(b.astype(jnp.float32))
    L = CHUNK if T % CHUNK == 0 else T
    nc = T // L

    def to_blocks(t):
        return t.reshape(B, nc, L, GDN_HEADS, -1).transpose(1, 0, 3, 2, 4)

    def to_blocks_h(t):
        return t.reshape(B, nc, L, GDN_HEADS).transpose(1, 0, 3, 2)

    S_final, o = lax.scan(_gdn_block, S0.astype(jnp.float32),
                          (to_blocks(q), to_blocks(k), to_blocks(v), to_blocks_h(g), to_blocks_h(beta)))
    o = o.transpose(1, 0, 3, 2, 4).reshape(B, T, GDN_HEADS, GDN_DV)
    o = _rms_norm(o, norm_w) * jax.nn.silu(z.reshape(B, T, GDN_HEADS, GDN_DV).astype(jnp.float32))
    return o.astype(qkv.dtype), new_buf, S_final


def _token_mixer(x, fox_hist, conv_buf, S0, attn_norm_w, w_in, fox_f_bias, gdn_conv_w, gdn_a_log,
                 gdn_dt_bias, gdn_norm_w, gate_bias, fox_w_o, gdn_w_o, w_out):
    B, T, _ = x.shape
    h = _rms_norm(x, attn_norm_w)
    p = h @ w_in
    fq, fk, fv, ff, gqkv, gz, ga, gb, ma, mb = _split_columns(p)
    fq = fq.reshape(B, T, FOX_HEADS, FOX_HEAD_DIM)
    fk = fk.reshape(B, T, FOX_HEADS, FOX_HEAD_DIM)
    fv = fv.reshape(B, T, FOX_HEADS, FOX_HEAD_DIM)
    logf = jax.nn.log_sigmoid((ff + fox_f_bias).astype(jnp.float32))
    if fox_hist is None:
        fo = _fox_prompt(fq, fk, fv, logf)
    else:
        fo = _fox_step(fq, fk, fv, logf, fox_hist[0], fox_hist[1], fox_hist[2])
    go, conv_new, S_new = _gated_delta(gqkv, gz, ga, gb, conv_buf, S0, gdn_conv_w, gdn_a_log,
                                       gdn_dt_bias, gdn_norm_w)
    ya = fo.reshape(B, T, FOX_W) @ fox_w_o
    yb = go.reshape(B, T, GDN_VW) @ gdn_w_o
    merged = jax.nn.sigmoid(ma + gate_bias[0]) * ya + jax.nn.sigmoid(mb + gate_bias[1]) * yb
    return x + merged @ w_out, fk, fv, logf, S_new, conv_new


def _moe_ffn(x, norm_w, router_w, router_b, w_gu, b_gu, w_down, b_down):
    B, T, D = x.shape
    N = B * T
    h = _rms_norm(x, norm_w).reshape(N, D)
    logits = (h @ router_w + router_b).astype(jnp.float32)
    top_val, top_idx = lax.top_k(logits, TOP_K)
    gates = jax.nn.softmax(top_val, axis=-1)
    nk = N * TOP_K
    n_blocks = -(-nk // MOE_BLOCK) + N_EXPERTS
    rows = n_blocks * MOE_BLOCK
    flat_e = top_idx.reshape(nk).astype(jnp.int32)
    flat_tok = jnp.arange(nk, dtype=jnp.int32) // TOP_K
    order = jnp.argsort(flat_e)
    sorted_e = flat_e[order]
    counts = jnp.bincount(flat_e, length=N_EXPERTS).astype(jnp.int32)
    padded = (counts + MOE_BLOCK - 1) // MOE_BLOCK * MOE_BLOCK
    start = jnp.cumsum(counts) - counts
    pad_end = jnp.cumsum(padded)
    pad_start = pad_end - padded
    dest = pad_start[sorted_e] + jnp.arange(nk, dtype=jnp.int32) - start[sorted_e]
    slot_tok = jnp.full((rows,), N, jnp.int32).at[dest].set(flat_tok[order])
    block_e = jnp.minimum(jnp.searchsorted(pad_end, jnp.arange(n_blocks, dtype=jnp.int32) * MOE_BLOCK,
                                           side='right'), N_EXPERTS - 1)
    h_pad = jnp.concatenate([h, jnp.zeros((1, D), h.dtype)], axis=0)
    xb = h_pad[slot_tok].reshape(n_blocks, MOE_BLOCK, D)

    def expert_block(args):
        xe, e = args
        gu = xe @ w_gu[e] + b_gu[e]
        gate = jnp.minimum(gu[:, :D_FF], SWIGLU_LIMIT)
        up = jnp.clip(gu[:, D_FF:], -SWIGLU_LIMIT, SWIGLU_LIMIT)
        act = (up + 1.0) * (gate * jax.nn.sigmoid(SWIGLU_ALPHA * gate))
        return act @ w_down[e] + b_down[e]

    yb = lax.map(expert_block, (xb, block_e)).reshape(rows, D)
    slot_of_assign = jnp.zeros((nk,), jnp.int32).at[order].set(dest)
    y = yb[slot_of_assign].reshape(N, TOP_K, D)
    out = jnp.einsum('nk,nkd->nd', gates.astype(y.dtype), y)
    return x + out.reshape(B, T, D)


def setup_inputs(seed: int = 0) -> dict:
    key = jax.random.key(seed)
    ks = jax.random.split(key, 32)
    f32 = jnp.float32

    def nrm(k, shape, scale):
        return jax.random.normal(k, shape, f32) * scale

    x_prompt = nrm(ks[0], (BATCH, SEQ, D_MODEL), 1.0)
    x_sample = nrm(ks[1], (DEC_BATCH, DEC_SEQ, D_MODEL), 1.0)
    cache_fox_k = nrm(ks[2], (DEPTH, DEC_BATCH, PAST_LEN, FOX_HEADS, FOX_HEAD_DIM), 1.0)
    cache_fox_v = nrm(ks[3], (DEPTH, DEC_BATCH, PAST_LEN, FOX_HEADS, FOX_HEAD_DIM), 1.0)
    cache_fox_logf = jax.nn.log_sigmoid(FORGET_BIAS_INIT + nrm(ks[4], (DEPTH, DEC_BATCH, PAST_LEN, FOX_HEADS), 1.0))
    state_gdn = nrm(ks[5], (DEPTH, DEC_BATCH, GDN_HEADS, GDN_DK, GDN_DV), 0.1)
    state_gdn_conv = nrm(ks[6], (DEPTH, DEC_BATCH, CONV_W - 1, GDN_CONV_CH), 1.0)
    attn_norm_w = 1.0 + nrm(ks[7], (DEPTH, D_MODEL), 0.05)
    w_in = nrm(ks[8], (DEPTH, D_MODEL, IN_W), D_MODEL ** -0.5)
    fox_f_bias = FORGET_BIAS_INIT + nrm(ks[9], (DEPTH, FOX_HEADS), 0.5)
    gdn_conv_w = nrm(ks[10], (DEPTH, CONV_W, GDN_CONV_CH), CONV_W ** -0.5)
    gdn_a_log = jnp.log(jax.random.uniform(ks[11], (DEPTH, GDN_HEADS), f32, 1.0, 16.0))
    dt = jnp.exp(jax.random.uniform(ks[12], (DEPTH, GDN_HEADS), f32, math.log(1e-3), math.log(1e-1)))
    gdn_dt_bias = dt + jnp.log(-jnp.expm1(-dt))
    gdn_norm_w = 1.0 + nrm(ks[13], (DEPTH, GDN_DV), 0.05)
    gate_bias = nrm(ks[14], (DEPTH, 2, D_MODEL), 0.01)
    fox_w_o = nrm(ks[15], (DEPTH, FOX_W, D_MODEL), FOX_W ** -0.5)
    gdn_w_o = nrm(ks[16], (DEPTH, GDN_VW, D_MODEL), GDN_VW ** -0.5)
    w_out = nrm(ks[17], (DEPTH, D_MODEL, D_MODEL), D_MODEL ** -0.5)
    ffn_norm_w = 1.0 + nrm(ks[18], (DEPTH, D_MODEL), 0.05)
    router_w = nrm(ks[19], (DEPTH, D_MODEL, N_EXPERTS), D_MODEL ** -0.5)
    router_b = nrm(ks[20], (DEPTH, N_EXPERTS), 0.01)
    expert_w_gu = nrm(ks[21], (DEPTH, N_EXPERTS, D_MODEL, 2 * D_FF), D_MODEL ** -0.5)
    expert_b_gu = nrm(ks[22], (DEPTH, N_EXPERTS, 2 * D_FF), 0.01)
    expert_w_down = nrm(ks[23], (DEPTH, N_EXPERTS, D_FF, D_MODEL), D_FF ** -0.5)
    expert_b_down = nrm(ks[24], (DEPTH, N_EXPERTS, D_MODEL), 0.01)
    final_norm_w = 1.0 + nrm(ks[25], (D_MODEL,), 0.05)
    return {'x_prompt': x_prompt, 'x_sample': x_sample,
            'cache_fox_k': cache_fox_k, 'cache_fox_v': cache_fox_v, 'cache_fox_logf': cache_fox_logf,
            'state_gdn': state_gdn, 'state_gdn_conv': state_gdn_conv,
            'attn_norm_w': attn_norm_w, 'w_in': w_in, 'fox_f_bias': fox_f_bias, 'gdn_conv_w': gdn_conv_w,
            'gdn_a_log': gdn_a_log, 'gdn_dt_bias': gdn_dt_bias, 'gdn_norm_w': gdn_norm_w,
            'gate_bias': gate_bias, 'fox_w_o': fox_w_o, 'gdn_w_o': gdn_w_o, 'w_out': w_out,
            'ffn_norm_w': ffn_norm_w, 'router_w': router_w, 'router_b': router_b,
            'expert_w_gu': expert_w_gu, 'expert_b_gu': expert_b_gu,
            'expert_w_down': expert_w_down, 'expert_b_down': expert_b_down,
            'final_norm_w': final_norm_w}


def reference(x_prompt, x_sample, cache_fox_k, cache_fox_v, cache_fox_logf, state_gdn, state_gdn_conv,
              attn_norm_w, w_in, fox_f_bias, gdn_conv_w, gdn_a_log, gdn_dt_bias, gdn_norm_w, gate_bias,
              fox_w_o, gdn_w_o, w_out, ffn_norm_w, router_w, router_b, expert_w_gu, expert_b_gu,
              expert_w_down, expert_b_down, final_norm_w):
    xp, xs = x_prompt, x_sample
    pk, pv, pf, ps, pc = [], [], [], [], []
    sk, sv, sf, ss, sc = [], [], [], [], []
    for l in range(DEPTH):
        mix_w = (attn_norm_w[l], w_in[l], fox_f_bias[l], gdn_conv_w[l], gdn_a_log[l], gdn_dt_bias[l],
                 gdn_norm_w[l], gate_bias[l], fox_w_o[l], gdn_w_o[l], w_out[l])
        ffn_w = (ffn_norm_w[l], router_w[l], router_b[l], expert_w_gu[l], expert_b_gu[l],
                 expert_w_down[l], expert_b_down[l])
        bp = xp.shape[0]
        zero_buf = jnp.zeros((bp, CONV_W - 1, GDN_CONV_CH), xp.dtype)
        zero_state = jnp.zeros((bp, GDN_HEADS, GDN_DK, GDN_DV), jnp.float32)
        xp, k_, v_, f_, s_, c_ = _token_mixer(xp, None, zero_buf, zero_state, *mix_w)
        xp = _moe_ffn(xp, *ffn_w)
        pk.append(k_); pv.append(v_); pf.append(f_); ps.append(s_); pc.append(c_)
        xs, k_, v_, f_, s_, c_ = _token_mixer(xs, (cache_fox_k[l], cache_fox_v[l], cache_fox_logf[l]),
                                              state_gdn_conv[l], state_gdn[l], *mix_w)
        xs = _moe_ffn(xs, *ffn_w)
        sk.append(k_); sv.append(v_); sf.append(f_); ss.append(s_); sc.append(c_)
    y_prompt = _rms_norm(xp, final_norm_w)
    y_sample = _rms_norm(xs, final_norm_w)
    return (y_prompt, y_sample,
            jnp.stack(pk), jnp.stack(pv), jnp.stack(pf), jnp.stack(ps), jnp.stack(pc),
            jnp.stack(sk), jnp.stack(sv), jnp.stack(sf), jnp.stack(ss), jnp.stack(sc))
```

```python
import functools

import numpy as np
import jax
import jax.numpy as jnp
from jax import lax
from jax.experimental import pallas as pl
from jax.experimental.pallas import tpu as pltpu

F32 = jnp.float32
BF16 = jnp.bfloat16
HIGHEST = lax.Precision.HIGHEST

D_MODEL = 1024
FOX_HEADS = 8
FOX_HEAD_DIM = 64
FOX_W = FOX_HEADS * FOX_HEAD_DIM
FOX_SCALE = FOX_HEAD_DIM ** -0.5
GDN_HEADS = 4
GDN_DK = 128
GDN_DV = 128
GDN_KW = GDN_HEADS * GDN_DK
GDN_VW = GDN_HEADS * GDN_DV
GDN_CONV_CH = 2 * GDN_KW + GDN_VW
CONV_W = 4
CHUNK = 64
N_EXPERTS = 32
TOP_K = 4
D_FF = D_MODEL
SWIGLU_LIMIT = 7.0
SWIGLU_ALPHA = 1.702
RMS_EPS = 1e-6
NEG_INF = -1e30

LANES = 128
SMALL_W = LANES
LOGF_LO, G_LO, BETA_LO = 0, FOX_HEADS, FOX_HEADS + GDN_HEADS
C_Q, C_K, C_V = 0, FOX_W, 2 * FOX_W
C_GQKV = 3 * FOX_W
C_GZ = C_GQKV + GDN_CONV_CH
C_MA = C_GZ + GDN_VW
C_MB = C_MA + D_MODEL
C_SMALL = C_MB + D_MODEL
W_R = C_SMALL + SMALL_W

VMEM_LIMIT = 56 * 1024 * 1024
EXPERT_BLOCK = 512
FF_CHUNK = 512


def _cparams(sem, vmem=VMEM_LIMIT):
    return pltpu.CompilerParams(dimension_semantics=sem, vmem_limit_bytes=vmem)


def _softplus(x):
    return jnp.maximum(x, 0.0) + jnp.log1p(jnp.exp(-jnp.abs(x)))


def _sigmoid(x):
    return jax.nn.sigmoid(x)


def _dot(a, b, precision=None):
    return jnp.dot(a, b, preferred_element_type=F32, precision=precision)


def _dot_nt(a, b, precision=None):
    return lax.dot_general(a, b, (((1,), (1,)), ((), ())), preferred_element_type=F32, precision=precision)


def _dot_tn(a, b, precision=None):
    return lax.dot_general(a, b, (((0,), (0,)), ((), ())), preferred_element_type=F32, precision=precision)


def _proj_kernel(x_ref, nw_ref, w_ref, sb_ref, q_ref, k_ref, v_ref, kb_ref, vb_ref, gqkv_ref, gz_ref, ma_ref,
                 mb_ref, small_ref):
    x = x_ref[...]
    ms = jnp.mean(x * x, axis=-1, keepdims=True)
    h = ((x * lax.rsqrt(ms + RMS_EPS)) * nw_ref[...]).astype(BF16)

    def sec(lo, width):
        return _dot(h, w_ref[:, lo:lo + width])

    q_ref[...] = (sec(C_Q, FOX_W) * FOX_SCALE).astype(BF16)
    k = sec(C_K, FOX_W)
    k_ref[...] = k
    kb_ref[...] = k.astype(BF16)
    v = sec(C_V, FOX_W)
    v_ref[...] = v
    vb_ref[...] = v.astype(BF16)
    gqkv_ref[...] = sec(C_GQKV, GDN_CONV_CH)
    gz_ref[...] = sec(C_GZ, GDN_VW)
    ma_ref[...] = sec(C_MA, D_MODEL)
    mb_ref[...] = sec(C_MB, D_MODEL)
    z = sec(C_SMALL, SMALL_W) + sb_ref[0:1, :]
    lane = lax.broadcasted_iota(jnp.int32, z.shape, 1)
    logf = -_softplus(-z)
    g = -jnp.exp(sb_ref[1:2, :]) * _softplus(z)
    beta = _sigmoid(z)
    small_ref[...] = jnp.where(lane < G_LO, logf, jnp.where(lane < BETA_LO, g, beta))


def _proj(x2d, nw, w_r, sb, tm):
    n = x2d.shape[0]
    row = lambda w: pl.BlockSpec((tm, w), lambda i: (i, 0))
    const = lambda shape: pl.BlockSpec(shape, lambda i: (0, 0))
    outs = [(FOX_W, BF16), (FOX_W, F32), (FOX_W, F32), (FOX_W, BF16), (FOX_W, BF16), (GDN_CONV_CH, F32),
            (GDN_VW, F32), (D_MODEL, F32), (D_MODEL, F32), (SMALL_W, F32)]
    return pl.pallas_call(
        _proj_kernel,
        grid=(n // tm,),
        in_specs=[row(D_MODEL), const((1, D_MODEL)), const((D_MODEL, W_R)), const((8, SMALL_W))],
        out_specs=[row(w) for w, _ in outs],
        out_shape=[jax.ShapeDtypeStruct((n, w), dt) for w, dt in outs],
        compiler_params=_cparams(("parallel",)),
        name="proj",
    )(x2d, nw, w_r, sb)


def _cumsum_kernel(x_ref, tri_ref, lmat_ref, c_ref):
    x = x_ref[...]
    cb = _dot(x, tri_ref[...], HIGHEST)
    tot = jnp.broadcast_to(cb[:, LANES - 1:LANES], cb.shape)
    c_ref[...] = cb + _dot(lmat_ref[...], tot, HIGHEST)


def _seq_cumsum(logf_bhs):
    b, h, s = logf_bhs.shape
    nb = s // LANES
    r = h * nb
    tri = jnp.asarray(np.triu(np.ones((LANES, LANES), np.float32)))
    rr = np.arange(r)
    lmat = jnp.asarray(((rr[:, None] // nb == rr[None, :] // nb) & (rr[None, :] < rr[:, None])).astype(np.float32))
    out = pl.pallas_call(
        _cumsum_kernel,
        grid=(b,),
        in_specs=[pl.BlockSpec((None, r, LANES), lambda i: (i, 0, 0)),
                  pl.BlockSpec((LANES, LANES), lambda i: (0, 0)),
                  pl.BlockSpec((r, r), lambda i: (0, 0))],
        out_specs=pl.BlockSpec((None, r, LANES), lambda i: (i, 0, 0)),
        out_shape=jax.ShapeDtypeStruct((b, r, LANES), F32),
        compiler_params=_cparams(("parallel",)),
        name="logf_cumsum",
    )(logf_bhs.reshape(b, r, LANES), tri, lmat)
    return out.reshape(b, h, s)


def _attn_prompt_kernel(q_ref, k_ref, v_ref, c_ref, o_ref, *, tq):
    i = pl.program_id(1)
    q0 = pl.multiple_of(i * tq, tq)
    lane = lax.broadcasted_iota(jnp.int32, (1, LANES), 1)
    rowi = lax.broadcasted_iota(jnp.int32, (tq, tq), 0)
    coli = lax.broadcasted_iota(jnp.int32, (tq, tq), 1)
    causal = rowi >= coli
    for pair in range(FOX_HEADS // 2):
        cols = slice(pair * LANES, (pair + 1) * LANES)
        q2 = q_ref[:, cols]
        outs = []
        for sub in range(2):
            h = 2 * pair + sub
            qm = jnp.where((lane >= sub * FOX_HEAD_DIM) & (lane < (sub + 1) * FOX_HEAD_DIM), q2, jnp.zeros_like(q2))
            c_q = c_ref[h:h + 1, pl.ds(q0, tq)]
            c0 = c_q[:, 0:1]

            def step(k2, v2, c_k, carry, mask):
                m, l, acc = carry
                s = _dot_nt(qm, k2) + (c0 - c_k)
                if mask:
                    s = jnp.where(causal, s, NEG_INF)
                m_new = jnp.maximum(m, jnp.max(s, axis=-1, keepdims=True))
                alpha = jnp.exp(m - m_new)
                p = jnp.exp(s - m_new)
                l = alpha * l + jnp.sum(p, axis=-1, keepdims=True)
                acc = alpha * acc + _dot(p.astype(BF16), v2)
                return m_new, l, acc

            def body(j, carry):
                k0 = pl.multiple_of(j * tq, tq)
                return step(k_ref[pl.ds(k0, tq), cols], v_ref[pl.ds(k0, tq), cols],
                            c_ref[h:h + 1, pl.ds(k0, tq)], carry, False)

            init = (jnp.full((tq, 1), NEG_INF, F32), jnp.zeros((tq, 1), F32), jnp.zeros((tq, LANES), F32))
            carry = lax.fori_loop(0, i, body, init)
            m, l, acc = step(k_ref[pl.ds(q0, tq), cols], v_ref[pl.ds(q0, tq), cols], c_q, carry, True)
            outs.append(acc * (1.0 / l))
        o_ref[:, cols] = jnp.where(lane < FOX_HEAD_DIM, outs[0], outs[1]).astype(BF16)


def _attn_prompt(qb, kb, vb, c, tq):
    b, s, _ = qb.shape
    return pl.pallas_call(
        functools.partial(_attn_prompt_kernel, tq=tq),
        grid=(b, s // tq),
        in_specs=[pl.BlockSpec((None, tq, FOX_W), lambda bi, i: (bi, i, 0)),
                  pl.BlockSpec((None, s, FOX_W), lambda bi, i: (bi, 0, 0)),
                  pl.BlockSpec((None, s, FOX_W), lambda bi, i: (bi, 0, 0)),
                  pl.BlockSpec((None, FOX_HEADS, s), lambda bi, i: (bi, 0, 0))],
        out_specs=pl.BlockSpec((None, tq, FOX_W), lambda bi, i: (bi, i, 0)),
        out_shape=jax.ShapeDtypeStruct((b, s, FOX_W), BF16),
        compiler_params=_cparams(("parallel", "arbitrary")),
        name="attn_prompt",
    )(qb, kb, vb, c)


def _attn_step_kernel(q_ref, kn_ref, vn_ref, kc_ref, vc_ref, c_ref, o_ref, *, past, t):
    lane = lax.broadcasted_iota(jnp.int32, (1, LANES), 1)
    rowi = lax.broadcasted_iota(jnp.int32, (t, t), 0)
    coli = lax.broadcasted_iota(jnp.int32, (t, t), 1)
    causal = rowi >= coli
    for pair in range(FOX_HEADS // 2):
        cols = slice(pair * LANES, (pair + 1) * LANES)
        q2 = q_ref[:, cols]
        kc = kc_ref[:, cols].astype(BF16)
        vc = vc_ref[:, cols].astype(BF16)
        kn = kn_ref[:, cols]
        vn = vn_ref[:, cols]
        outs = []
        for sub in range(2):
            h = 2 * pair + sub
            qm = jnp.where((lane >= sub * FOX_HEAD_DIM) & (lane < (sub + 1) * FOX_HEAD_DIM), q2, jnp.zeros_like(q2))
            c_c = c_ref[h:h + 1, 0:past]
            c_n = c_ref[h:h + 1, past:past + t]
            c0 = c_n[:, 0:1]
            s_c = _dot_nt(qm, kc) + (c0 - c_c)
            s_n = jnp.where(causal, _dot_nt(qm, kn) + (c0 - c_n), NEG_INF)
            m = jnp.maximum(jnp.max(s_c, axis=-1, keepdims=True), jnp.max(s_n, axis=-1, keepdims=True))
            p_c = jnp.exp(s_c - m)
            p_n = jnp.exp(s_n - m)
            l = jnp.sum(p_c, axis=-1, keepdims=True) + jnp.sum(p_n, axis=-1, keepdims=True)
            acc = _dot(p_c.astype(BF16), vc) + _dot(p_n.astype(BF16), vn)
            outs.append(acc * (1.0 / l))
        o_ref[:, cols] = jnp.where(lane < FOX_HEAD_DIM, outs[0], outs[1]).astype(BF16)


def _attn_step(qb, knb, vnb, k_cache, v_cache, c_all):
    b, t, _ = qb.shape
    past = k_cache.shape[1]
    sp = c_all.shape[-1]
    new = pl.BlockSpec((None, t, FOX_W), lambda bi: (bi, 0, 0))
    old = pl.BlockSpec((None, past, FOX_W), lambda bi: (bi, 0, 0))
    return pl.pallas_call(
        functools.partial(_attn_step_kernel, past=past, t=t),
        grid=(b,),
        in_specs=[new, new, new, old, old, pl.BlockSpec((None, FOX_HEADS, sp), lambda bi: (bi, 0, 0))],
        out_specs=new,
        out_shape=jax.ShapeDtypeStruct((b, t, FOX_W), BF16),
        compiler_params=_cparams(("parallel",)),
        name="attn_step",
    )(qb, knb, vnb, k_cache, v_cache, c_all)


def _unit_lower_inverse(a, n):
    eye = (lax.broadcasted_iota(jnp.int32, (n, n), 0) == lax.broadcasted_iota(jnp.int32, (n, n), 1)).astype(F32)
    x = eye - a
    p = a
    m = 2
    while m < n:
        p = _dot(p, p, HIGHEST)
        x = x + _dot(x, p, HIGHEST)
        m *= 2
    return x


def _gdn_kernel(x_ref, z_ref, sm_ref, cb_ref, s0_ref, cw_ref, nw_ref, o_ref, s_ref, xbuf, *, rows):
    c = pl.program_id(1)
    L = CHUNK

    @pl.when(c == 0)
    def _():
        xbuf[8 - (CONV_W - 1):8, :] = cb_ref[...]
        s_ref[...] = s0_ref[...]

    xbuf[8:8 + rows, :] = x_ref[...]
    conv = xbuf[5:5 + rows, :] * cw_ref[0:1, :]
    for i in range(1, CONV_W):
        conv = conv + xbuf[5 + i:5 + i + rows, :] * cw_ref[i:i + 1, :]
    xbuf[5:8, :] = xbuf[rows + 5:rows + 8, :]
    conv = conv * _sigmoid(conv)

    ri = lax.broadcasted_iota(jnp.int32, (L, L), 0)
    ci = lax.broadcasted_iota(jnp.int32, (L, L), 1)
    tri_incl = ri >= ci
    tri_strict = ri > ci
    eye = ri == ci
    tril_f = tri_incl.astype(F32)

    for ch in range(rows // L):
        r0 = ch * L
        sm = sm_ref[r0:r0 + L, :]
        gcum = _dot(tril_f, sm, HIGHEST)
        for h in range(GDN_HEADS):
            xq = conv[r0:r0 + L, h * GDN_DK:(h + 1) * GDN_DK]
            xk = conv[r0:r0 + L, GDN_KW + h * GDN_DK:GDN_KW + (h + 1) * GDN_DK]
            v = conv[r0:r0 + L, 2 * GDN_KW + h * GDN_DV:2 * GDN_KW + (h + 1) * GDN_DV]
            q = xq * lax.rsqrt(jnp.sum(xq * xq, axis=-1, keepdims=True) + RMS_EPS) * (GDN_DK ** -0.5)
            k = xk * lax.rsqrt(jnp.sum(xk * xk, axis=-1, keepdims=True) + RMS_EPS)
            beta = sm[:, BETA_LO + h:BETA_LO + h + 1]
            gc = gcum[:, G_LO + h:G_LO + h + 1]
            grow = jnp.sum(jnp.where(eye, gc, 0.0), axis=0, keepdims=True)
            decay = jnp.where(tri_incl, jnp.exp(jnp.where(tri_incl, gc - grow, 0.0)), 0.0)
            eg = jnp.exp(gc)
            glast = gc[L - 1:L, :]
            kb = k.astype(BF16)
            qb = q.astype(BF16)
            kk = _dot_nt(kb, kb)
            a = jnp.where(tri_strict, kk * decay, 0.0) * beta
            tinv = _unit_lower_inverse(a, L)
            s = s_ref[h]
            sb = s.astype(BF16)
            rhs = beta * (v - eg * _dot(kb, sb))
            delta = _dot(tinv, rhs, HIGHEST)
            db = delta.astype(BF16)
            qk = _dot_nt(qb, kb) * decay
            o = eg * _dot(qb, sb) + _dot(qk.astype(BF16), db)
            kd = (k * jnp.exp(glast - gc)).astype(BF16)
            s_ref[h] = jnp.exp(glast) * s + _dot_tn(kd, db)
            on = o * lax.rsqrt(jnp.mean(o * o, axis=-1, keepdims=True) + RMS_EPS) * nw_ref[...]
            zz = z_ref[r0:r0 + L, h * GDN_DV:(h + 1) * GDN_DV]
            o_ref[r0:r0 + L, h * GDN_DV:(h + 1) * GDN_DV] = (on * (zz * _sigmoid(zz))).astype(BF16)


def _gdn(gqkv, gz, small, conv_buf, s0, conv_w, norm_w, rows):
    b, t, _ = gqkv.shape
    blk = lambda w: pl.BlockSpec((None, rows, w), lambda bi, c: (bi, c, 0))
    state = pl.BlockSpec((None, GDN_HEADS, GDN_DK, GDN_DV), lambda bi, c: (bi, 0, 0, 0))
    return pl.pallas_call(
        functools.partial(_gdn_kernel, rows=rows),
        grid=(b, t // rows),
        in_specs=[blk(GDN_CONV_CH), blk(GDN_VW), blk(SMALL_W),
                  pl.BlockSpec((None, CONV_W - 1, GDN_CONV_CH), lambda bi, c: (bi, 0, 0)),
                  state,
                  pl.BlockSpec((CONV_W, GDN_CONV_CH), lambda bi, c: (0, 0)),
                  pl.BlockSpec((1, GDN_DV), lambda bi, c: (0, 0))],
        out_specs=[blk(GDN_VW), state],
        out_shape=[jax.ShapeDtypeStruct((b, t, GDN_VW), BF16),
                   jax.ShapeDtypeStruct((b, GDN_HEADS, GDN_DK, GDN_DV), F32)],
        scratch_shapes=[pltpu.VMEM((rows + 8, GDN_CONV_CH), F32)],
        compiler_params=_cparams(("parallel", "arbitrary")),
        name="gdn",
    )(gqkv, gz, small, conv_buf, s0, conv_w, norm_w)


def _post_kernel(fo_ref, go_ref, ma_ref, mb_ref, x_ref, wfo_ref, wgo_ref, wout_ref, gb_ref, nw_ref, rw_ref, rb_ref,
                 tri_ref, cin_ref, x2_ref, h2_ref, gates_ref, ir_ref, cnt_ref, carry):
    i = pl.program_id(0)

    @pl.when(i == 0)
    def _():
        carry[...] = cin_ref[...]

    ya = _dot(fo_ref[...], wfo_ref[...])
    yb = _dot(go_ref[...], wgo_ref[...])
    merged = _sigmoid(ma_ref[...] + gb_ref[0:1, :]) * ya + _sigmoid(mb_ref[...] + gb_ref[1:2, :]) * yb
    x2 = x_ref[...] + _dot(merged.astype(BF16), wout_ref[...])
    x2_ref[...] = x2
    h2 = (x2 * lax.rsqrt(jnp.mean(x2 * x2, axis=-1, keepdims=True) + RMS_EPS)) * nw_ref[...]
    h2_ref[...] = h2
    logits = _dot(h2, rw_ref[...], HIGHEST) + rb_ref[...]
    lane = lax.broadcasted_iota(jnp.int32, logits.shape, 1).astype(F32)
    work = logits
    vals, hits = [], []
    for _ in range(TOP_K):
        m = jnp.max(work, axis=-1, keepdims=True)
        idx = jnp.min(jnp.where(work == m, lane, float(LANES)), axis=-1, keepdims=True)
        hit = lane == idx
        vals.append(m)
        hits.append((hit, idx))
        work = jnp.where(hit, -jnp.inf, work)
    es = [jnp.exp(v - vals[0]) for v in vals]
    denom = es[0] + es[1] + es[2] + es[3]
    cnt = jnp.zeros(logits.shape, F32)
    for hit, _ in hits:
        cnt = cnt + hit.astype(F32)
    base = _dot(tri_ref[...], cnt.astype(BF16)) + carry[...]
    gates = jnp.zeros(logits.shape, F32)
    ir = jnp.zeros(logits.shape, F32)
    for kk, (hit, idx) in enumerate(hits):
        rank = jnp.sum(jnp.where(hit, base, 0.0), axis=-1, keepdims=True)
        gates = gates + jnp.where(lane == float(kk), es[kk] / denom, 0.0)
        ir = ir + jnp.where(lane == float(kk), idx, 0.0) + jnp.where(lane == float(TOP_K + kk), rank, 0.0)
    gates_ref[...] = gates
    ir_ref[...] = ir.astype(jnp.int32)
    carry[...] = carry[...] + jnp.sum(cnt, axis=0, keepdims=True)
    cnt_ref[...] = carry[...]


def _post(fo, go, ma, mb, x, wfo, wgo, wout, gate_bias, nw, rw, rb, cnt_in, tm):
    n = x.shape[0]
    row = lambda w: pl.BlockSpec((tm, w), lambda i: (i, 0))
    const = lambda shape: pl.BlockSpec(shape, lambda i: (0, 0))
    tri = jnp.asarray(np.tril(np.ones((tm, tm), np.float32), -1), BF16)
    return pl.pallas_call(
        _post_kernel,
        grid=(n // tm,),
        in_specs=[row(FOX_W), row(GDN_VW), row(D_MODEL), row(D_MODEL), row(D_MODEL),
                  const((FOX_W, D_MODEL)), const((GDN_VW, D_MODEL)), const((D_MODEL, D_MODEL)),
                  const((2, D_MODEL)), const((1, D_MODEL)), const((D_MODEL, LANES)), const((1, LANES)),
                  const((tm, tm)), const((1, LANES))],
        out_specs=[row(D_MODEL), row(D_MODEL), row(LANES), row(LANES), const((1, LANES))],
        out_shape=[jax.ShapeDtypeStruct((n, D_MODEL), F32), jax.ShapeDtypeStruct((n, D_MODEL), F32),
                   jax.ShapeDtypeStruct((n, LANES), F32), jax.ShapeDtypeStruct((n, LANES), jnp.int32),
                   jax.ShapeDtypeStruct((1, LANES), F32)],
        scratch_shapes=[pltpu.VMEM((1, LANES), F32)],
        compiler_params=_cparams(("arbitrary",)),
        name="post_router",
    )(fo, go, ma, mb, x, wfo, wgo, wout, gate_bias, nw, rw, rb, tri, cnt_in)


def _dispatch_kernel(dest_ref, h_ref, xs_in_ref, xs_ref, sem, *, tm):
    del xs_in_ref

    def row_copy(t, d):
        return pltpu.make_async_copy(h_ref.at[pl.ds(t, 1), :], xs_ref.at[pl.ds(d, 1), :], sem)

    def issue(t, carry):
        for kk in range(TOP_K):
            row_copy(t, dest_ref[t * TOP_K + kk]).start()
        return carry

    lax.fori_loop(0, tm, issue, 0)

    def drain(t, carry):
        for kk in range(TOP_K):
            row_copy(0, 0).wait()
        return carry

    lax.fori_loop(0, tm, drain, 0)


def _dispatch(dest_flat, h, xs, tm):
    n = h.shape[0]
    return pl.pallas_call(
        functools.partial(_dispatch_kernel, tm=tm),
        grid=(n // tm,),
        in_specs=[pl.BlockSpec((tm * TOP_K,), lambda i: (i,), memory_space=pltpu.SMEM),
                  pl.BlockSpec((tm, D_MODEL), lambda i: (i, 0)),
                  pl.BlockSpec(memory_space=pl.ANY)],
        out_specs=pl.BlockSpec(memory_space=pl.ANY),
        out_shape=jax.ShapeDtypeStruct(xs.shape, xs.dtype),
        scratch_shapes=[pltpu.SemaphoreType.DMA(())],
        input_output_aliases={2: 0},
        compiler_params=_cparams(("arbitrary",)),
        name="moe_dispatch",
    )(dest_flat, h, xs)


def _expert_kernel(be_ref, nu_ref, x_ref, wgu_ref, bgu_ref, wd_ref, bd_ref, y_ref, wgu_b, wd_b):
    i = pl.program_id(0)
    changed = jnp.logical_or(i == 0, be_ref[i] != be_ref[jnp.maximum(i - 1, 0)])

    @pl.when(changed)
    def _():
        for f in range(0, 2 * D_FF, FF_CHUNK):
            wgu_b[:, f:f + FF_CHUNK] = wgu_ref[:, f:f + FF_CHUNK].astype(BF16)
        for f in range(0, D_FF, FF_CHUNK):
            wd_b[f:f + FF_CHUNK, :] = wd_ref[f:f + FF_CHUNK, :].astype(BF16)

    @pl.when(i < nu_ref[0])
    def _():
        x = x_ref[...].astype(BF16)
        acc = jnp.zeros(y_ref.shape, F32)
        for f in range(0, D_FF, FF_CHUNK):
            gate = _dot(x, wgu_b[:, f:f + FF_CHUNK]) + bgu_ref[:, f:f + FF_CHUNK]
            up = _dot(x, wgu_b[:, D_FF + f:D_FF + f + FF_CHUNK]) + bgu_ref[:, D_FF + f:D_FF + f + FF_CHUNK]
            gate = jnp.minimum(gate, SWIGLU_LIMIT)
            up = jnp.clip(up, -SWIGLU_LIMIT, SWIGLU_LIMIT)
            act = (up + 1.0) * (gate * _sigmoid(SWIGLU_ALPHA * gate))
            acc = acc + _dot(act.astype(BF16), wd_b[f:f + FF_CHUNK, :])
        y_ref[...] = acc + bd_ref[...]

    @pl.when(i >= nu_ref[0])
    def _():
        y_ref[...] = jnp.zeros(y_ref.shape, F32)


def _experts(block_e, n_used, xs, w_gu, b_gu, w_down, b_down):
    rows = xs.shape[0]
    tb = EXPERT_BLOCK
    grid_spec = pltpu.PrefetchScalarGridSpec(
        num_scalar_prefetch=2,
        grid=(rows // tb,),
        in_specs=[pl.BlockSpec((tb, D_MODEL), lambda i, be, nu: (i, 0)),
                  pl.BlockSpec((None, D_MODEL, 2 * D_FF), lambda i, be, nu: (be[i], 0, 0)),
                  pl.BlockSpec((None, 1, 2 * D_FF), lambda i, be, nu: (be[i], 0, 0)),
                  pl.BlockSpec((None, D_FF, D_MODEL), lambda i, be, nu: (be[i], 0, 0)),
                  pl.BlockSpec((None, 1, D_MODEL), lambda i, be, nu: (be[i], 0, 0))],
        out_specs=pl.BlockSpec((tb, D_MODEL), lambda i, be, nu: (i, 0)),
        scratch_shapes=[pltpu.VMEM((D_MODEL, 2 * D_FF), BF16), pltpu.VMEM((D_FF, D_MODEL), BF16)],
    )
    return pl.pallas_call(
        _expert_kernel,
        grid_spec=grid_spec,
        out_shape=jax.ShapeDtypeStruct((rows, D_MODEL), F32),
        compiler_params=_cparams(("arbitrary",)),
        name="moe_experts",
    )(block_e, n_used, xs, w_gu, b_gu.reshape(N_EXPERTS, 1, 2 * D_FF), w_down, b_down.reshape(N_EXPERTS, 1, D_MODEL))


def _combine_kernel(dest_ref, x2_ref, gates_ref, fw_ref, ys_ref, y_ref, buf, sem, *, tm):
    def row_copy(t, kk, d):
        return pltpu.make_async_copy(ys_ref.at[pl.ds(d, 1), :], buf.at[kk, pl.ds(t, 1), :], sem)

    def issue(t, carry):
        for kk in range(TOP_K):
            row_copy(t, kk, dest_ref[t * TOP_K + kk]).start()
        return carry

    lax.fori_loop(0, tm, issue, 0)

    def drain(t, carry):
        for kk in range(TOP_K):
            row_copy(0, 0, 0).wait()
        return carry

    lax.fori_loop(0, tm, drain, 0)
    gates = gates_ref[...]
    out = x2_ref[...]
    for kk in range(TOP_K):
        out = out + gates[:, kk:kk + 1] * buf[kk]
    y_ref[...] = (out * lax.rsqrt(jnp.mean(out * out, axis=-1, keepdims=True) + RMS_EPS)) * fw_ref[...]


def _combine(dest_flat, x2, gates, final_w, ys, tm):
    n = x2.shape[0]
    return pl.pallas_call(
        functools.partial(_combine_kernel, tm=tm),
        grid=(n // tm,),
        in_specs=[pl.BlockSpec((tm * TOP_K,), lambda i: (i,), memory_space=pltpu.SMEM),
                  pl.BlockSpec((tm, D_MODEL), lambda i: (i, 0)),
                  pl.BlockSpec((tm, LANES), lambda i: (i, 0)),
                  pl.BlockSpec((1, D_MODEL), lambda i: (0, 0)),
                  pl.BlockSpec(memory_space=pl.ANY)],
        out_specs=pl.BlockSpec((tm, D_MODEL), lambda i: (i, 0)),
        out_shape=jax.ShapeDtypeStruct((n, D_MODEL), F32),
        scratch_shapes=[pltpu.VMEM((TOP_K, tm, D_MODEL), F32), pltpu.SemaphoreType.DMA(())],
        compiler_params=_cparams(("arbitrary",)),
        name="moe_combine",
    )(dest_flat, x2, gates, final_w, ys)


def _row_tile(n, want):
    t = min(want, n)
    while n % t:
        t //= 2
    return t


def kernel(x_prompt, x_sample, cache_fox_k, cache_fox_v, cache_fox_logf, state_gdn, state_gdn_conv, attn_norm_w, w_in, fox_f_bias, gdn_conv_w, gdn_a_log, gdn_dt_bias, gdn_norm_w, gate_bias, fox_w_o, gdn_w_o, w_out, ffn_norm_w, router_w, router_b, expert_w_gu, expert_b_gu, expert_w_down, expert_b_down, final_norm_w):
    l = 0
    bp, sp, d = x_prompt.shape
    bs, ts, _ = x_sample.shape
    past = cache_fox_k.shape[2]
    n_p, n_s = bp * sp, bs * ts

    w = w_in[l]
    o_ff = 3 * FOX_W
    o_gqkv = o_ff + FOX_HEADS
    o_gz = o_gqkv + GDN_CONV_CH
    o_ga = o_gz + GDN_VW
    o_gb = o_ga + GDN_HEADS
    o_ma = o_gb + GDN_HEADS
    w_r = jnp.concatenate(
        [w[:, :o_ff], w[:, o_gqkv:o_gz], w[:, o_gz:o_ga], w[:, o_ma:], w[:, o_ff:o_gqkv], w[:, o_ga:o_ma],
         jnp.zeros((d, SMALL_W - FOX_HEADS - 2 * GDN_HEADS), w.dtype)], axis=1).astype(BF16)
    sb = jnp.zeros((8, SMALL_W), F32)
    sb = sb.at[0, LOGF_LO:LOGF_LO + FOX_HEADS].set(fox_f_bias[l])
    sb = sb.at[0, G_LO:G_LO + GDN_HEADS].set(gdn_dt_bias[l])
    sb = sb.at[1, G_LO:G_LO + GDN_HEADS].set(gdn_a_log[l])
    anw = attn_norm_w[l].reshape(1, d)
    wfo = fox_w_o[l].astype(BF16)
    wgo = gdn_w_o[l].astype(BF16)
    wout = w_out[l].astype(BF16)
    fnw = ffn_norm_w[l].reshape(1, d)
    rw = jnp.concatenate([router_w[l], jnp.zeros((d, LANES - N_EXPERTS), F32)], axis=1)
    rb = jnp.concatenate([router_b[l], jnp.full((LANES - N_EXPERTS,), -jnp.inf, F32)]).reshape(1, LANES)
    gnw = gdn_norm_w[l].reshape(1, GDN_DV)
    final_w = final_norm_w.reshape(1, d)

    def mixer(x2d, b, t):
        q, k, v, kb, vb, gqkv, gz, ma, mb, small = _proj(x2d, anw, w_r, sb, _row_tile(x2d.shape[0], 256))
        return dict(q=q.reshape(b, t, FOX_W), k=k, v=v, kb=kb.reshape(b, t, FOX_W), vb=vb.reshape(b, t, FOX_W),
                    gqkv=gqkv.reshape(b, t, GDN_CONV_CH), gz=gz.reshape(b, t, GDN_VW), ma=ma, mb=mb,
                    small=small.reshape(b, t, SMALL_W))

    pp = mixer(x_prompt.reshape(n_p, d), bp, sp)
    logf_p = pp["small"][:, :, LOGF_LO:LOGF_LO + FOX_HEADS]
    c_p = _seq_cumsum(jnp.transpose(logf_p, (0, 2, 1)))
    fo_p = _attn_prompt(pp["q"], pp["kb"], pp["vb"], c_p, _row_tile(sp, 512))
    rows_p = _row_tile(sp, 2 * CHUNK)
    go_p, state_p = _gdn(pp["gqkv"], pp["gz"], pp["small"], jnp.zeros((bp, CONV_W - 1, GDN_CONV_CH), F32),
                         jnp.zeros((bp, GDN_HEADS, GDN_DK, GDN_DV), F32), gdn_conv_w[l], gnw, rows_p)
    conv_p = pp["gqkv"][:, sp - (CONV_W - 1):, :]

    ps = mixer(x_sample.reshape(n_s, d), bs, ts)
    logf_s = ps["small"][:, :, LOGF_LO:LOGF_LO + FOX_HEADS]
    tot = past + ts
    tot_pad = -(-tot // LANES) * LANES
    logf_all = jnp.concatenate([cache_fox_logf[l].astype(F32), logf_s,
                                jnp.zeros((bs, tot_pad - tot, FOX_HEADS), F32)], axis=1)
    c_s = _seq_cumsum(jnp.transpose(logf_all, (0, 2, 1)))
    fo_s = _attn_step(ps["q"], ps["kb"], ps["vb"], cache_fox_k[l].reshape(bs, past, FOX_W),
                      cache_fox_v[l].reshape(bs, past, FOX_W), c_s)
    t_pad = -(-ts // CHUNK) * CHUNK
    padt = lambda a: jnp.pad(a, ((0, 0), (0, t_pad - ts), (0, 0)))
    go_s, state_s = _gdn(padt(ps["gqkv"]), padt(ps["gz"]), padt(ps["small"]), state_gdn_conv[l], state_gdn[l],
                         gdn_conv_w[l], gnw, CHUNK)
    go_s = go_s[:, :ts, :]
    conv_s = ps["gqkv"][:, ts - (CONV_W - 1):, :]

    tm_p = _row_tile(n_p, 256)
    tm_s = _row_tile(n_s, 256)
    x2_p, h2_p, gates_p, ir_p, cnt_p = _post(fo_p.reshape(n_p, FOX_W), go_p.reshape(n_p, GDN_VW), pp["ma"], pp["mb"],
                                             x_prompt.reshape(n_p, d), wfo, wgo, wout, gate_bias[l], fnw, rw, rb,
                                             jnp.zeros((1, LANES), F32), tm_p)
    x2_s, h2_s, gates_s, ir_s, cnt_s = _post(fo_s.reshape(n_s, FOX_W), go_s.reshape(n_s, GDN_VW), ps["ma"], ps["mb"],
                                             x_sample.reshape(n_s, d), wfo, wgo, wout, gate_bias[l], fnw, rw, rb,
                                             cnt_p, tm_s)

    tb = EXPERT_BLOCK
    nk = (n_p + n_s) * TOP_K
    n_blocks = -(-nk // tb) + N_EXPERTS
    counts = cnt_s[0, :N_EXPERTS].astype(jnp.int32)
    padded = (counts + tb - 1) // tb * tb
    pad_end = jnp.cumsum(padded)
    pad_start = pad_end - padded
    block_e = jnp.minimum(jnp.searchsorted(pad_end, jnp.arange(n_blocks, dtype=jnp.int32) * tb, side='right'),
                          N_EXPERTS - 1).astype(jnp.int32)
    n_used = (pad_end[-1:] // tb).astype(jnp.int32)
    dest = lambda ir: (pad_start[ir[:, :TOP_K]] + ir[:, TOP_K:2 * TOP_K]).reshape(-1)
    dest_p, dest_s = dest(ir_p), dest(ir_s)

    xs = jnp.zeros((n_blocks * tb, d), F32)
    xs = _dispatch(dest_p, h2_p, xs, tm_p)
    xs = _dispatch(dest_s, h2_s, xs, tm_s)
    ys = _experts(block_e, n_used, xs, expert_w_gu[l], expert_b_gu[l], expert_w_down[l], expert_b_down[l])
    y_p = _combine(dest_p, x2_p, gates_p, final_w, ys, tm_p)
    y_s = _combine(dest_s, x2_s, gates_s, final_w, ys, tm_s)

    hd = (FOX_HEADS, FOX_HEAD_DIM)
    return (y_p.reshape(bp, sp, d), y_s.reshape(bs, ts, d),
            pp["k"].reshape(1, bp, sp, *hd), pp["v"].reshape(1, bp, sp, *hd), logf_p[None],
            state_p[None], conv_p[None],
            ps["k"].reshape(1, bs, ts, *hd), ps["v"].reshape(1, bs, ts, *hd), logf_s[None],
            state_s[None], conv_s[None])
```

```python
import functools

import numpy as np
import jax
import jax.numpy as jnp
from jax import lax
from jax.experimental import pallas as pl
from jax.experimental.pallas import tpu as pltpu

F32 = jnp.float32
BF16 = jnp.bfloat16
HIGHEST = lax.Precision.HIGHEST

D_MODEL = 1024
FOX_HEADS = 8
FOX_HEAD_DIM = 64
FOX_W = FOX_HEADS * FOX_HEAD_DIM
FOX_SCALE = FOX_HEAD_DIM ** -0.5
GDN_HEADS = 4
GDN_DK = 128
GDN_DV = 128
GDN_KW = GDN_HEADS * GDN_DK
GDN_VW = GDN_HEADS * GDN_DV
GDN_CONV_CH = 2 * GDN_KW + GDN_VW
CONV_W = 4
CHUNK = 64
N_EXPERTS = 32
TOP_K = 4
D_FF = D_MODEL
SWIGLU_LIMIT = 7.0
SWIGLU_ALPHA = 1.702
RMS_EPS = 1e-6
NEG_INF = -1e30

LANES = 128
SMALL_W = LANES
LOGF_LO, G_LO, BETA_LO = 0, FOX_HEADS, FOX_HEADS + GDN_HEADS
C_Q, C_K, C_V = 0, FOX_W, 2 * FOX_W
C_GQKV = 3 * FOX_W
C_GZ = C_GQKV + GDN_CONV_CH
C_MA = C_GZ + GDN_VW
C_MB = C_MA + D_MODEL
C_SMALL = C_MB + D_MODEL
W_R = C_SMALL + SMALL_W

VMEM_LIMIT = 56 * 1024 * 1024
ATTN_TQ = 512
ATTN_TK = 512
ATTN_HEADS_PER_LOOP = 2
ROW_GROUP = 8
EXPERT_BLOCK = 512
FF_CHUNK = 512


def _cparams(sem, vmem=VMEM_LIMIT):
    return pltpu.CompilerParams(dimension_semantics=sem, vmem_limit_bytes=vmem)


def _softplus(x):
    return jnp.maximum(x, 0.0) + jnp.log1p(jnp.exp(-jnp.abs(x)))


def _sigmoid(x):
    return jax.nn.sigmoid(x)


def _dot(a, b, precision=None):
    return jnp.dot(a, b, preferred_element_type=F32, precision=precision)


def _dot_nt(a, b, precision=None):
    return lax.dot_general(a, b, (((1,), (1,)), ((), ())), preferred_element_type=F32, precision=precision)


def _dot_tn(a, b, precision=None):
    return lax.dot_general(a, b, (((0,), (0,)), ((), ())), preferred_element_type=F32, precision=precision)


def _proj_kernel(x_ref, nw_ref, w_ref, sb_ref, q_ref, k_ref, v_ref, kb_ref, vb_ref, gqkv_ref, gz_ref, ma_ref,
                 mb_ref, small_ref):
    x = x_ref[...]
    ms = jnp.mean(x * x, axis=-1, keepdims=True)
    h = ((x * lax.rsqrt(ms + RMS_EPS)) * nw_ref[...]).astype(BF16)

    def sec(lo, width):
        return _dot(h, w_ref[:, lo:lo + width])

    q_ref[...] = (sec(C_Q, FOX_W) * FOX_SCALE).astype(BF16)
    k = sec(C_K, FOX_W)
    k_ref[...] = k
    kb_ref[...] = k.astype(BF16)
    v = sec(C_V, FOX_W)
    v_ref[...] = v
    vb_ref[...] = v.astype(BF16)
    gqkv_ref[...] = sec(C_GQKV, GDN_CONV_CH)
    gz_ref[...] = sec(C_GZ, GDN_VW)
    ma_ref[...] = sec(C_MA, D_MODEL)
    mb_ref[...] = sec(C_MB, D_MODEL)
    z = sec(C_SMALL, SMALL_W) + sb_ref[0:1, :]
    lane = lax.broadcasted_iota(jnp.int32, z.shape, 1)
    logf = -_softplus(-z)
    g = -jnp.exp(sb_ref[1:2, :]) * _softplus(z)
    beta = _sigmoid(z)
    small_ref[...] = jnp.where(lane < G_LO, logf, jnp.where(lane < BETA_LO, g, beta))


def _proj(x2d, nw, w_r, sb, tm):
    n = x2d.shape[0]
    row = lambda w: pl.BlockSpec((tm, w), lambda i: (i, 0))
    const = lambda shape: pl.BlockSpec(shape, lambda i: (0, 0))
    outs = [(FOX_W, BF16), (FOX_W, F32), (FOX_W, F32), (FOX_W, BF16), (FOX_W, BF16), (GDN_CONV_CH, F32),
            (GDN_VW, F32), (D_MODEL, F32), (D_MODEL, F32), (SMALL_W, F32)]
    return pl.pallas_call(
        _proj_kernel,
        grid=(n // tm,),
        in_specs=[row(D_MODEL), const((1, D_MODEL)), const((D_MODEL, W_R)), const((8, SMALL_W))],
        out_specs=[row(w) for w, _ in outs],
        out_shape=[jax.ShapeDtypeStruct((n, w), dt) for w, dt in outs],
        compiler_params=_cparams(("parallel",)),
        name="proj",
    )(x2d, nw, w_r, sb)


def _cumsum_kernel(x_ref, tri_ref, lmat_ref, c_ref):
    x = x_ref[...]
    cb = _dot(x, tri_ref[...], HIGHEST)
    tot = jnp.broadcast_to(cb[:, LANES - 1:LANES], cb.shape)
    c_ref[...] = cb + _dot(lmat_ref[...], tot, HIGHEST)


def _seq_cumsum(logf_bhs):
    b, h, s = logf_bhs.shape
    nb = s // LANES
    r = h * nb
    tri = jnp.asarray(np.triu(np.ones((LANES, LANES), np.float32)))
    rr = np.arange(r)
    lmat = jnp.asarray(((rr[:, None] // nb == rr[None, :] // nb) & (rr[None, :] < rr[:, None])).astype(np.float32))
    out = pl.pallas_call(
        _cumsum_kernel,
        grid=(b,),
        in_specs=[pl.BlockSpec((None, r, LANES), lambda i: (i, 0, 0)),
                  pl.BlockSpec((LANES, LANES), lambda i: (0, 0)),
                  pl.BlockSpec((r, r), lambda i: (0, 0))],
        out_specs=pl.BlockSpec((None, r, LANES), lambda i: (i, 0, 0)),
        out_shape=jax.ShapeDtypeStruct((b, r, LANES), F32),
        compiler_params=_cparams(("parallel",)),
        name="logf_cumsum",
    )(logf_bhs.reshape(b, r, LANES), tri, lmat)
    return out.reshape(b, h, s)


def _attn_prompt_kernel(q_ref, k_ref, v_ref, c_ref, o_ref, *, tq, tk):
    i = pl.program_id(1)
    q0 = pl.multiple_of(i * tq, tq)
    n_full = (i * tq) // tk
    lane = lax.broadcasted_iota(jnp.int32, (1, LANES), 1)
    rowi = lax.broadcasted_iota(jnp.int32, (tq, tk), 0)
    coli = lax.broadcasted_iota(jnp.int32, (tq, tk), 1)
    for g0 in range(0, FOX_HEADS, ATTN_HEADS_PER_LOOP):
        heads = list(range(g0, g0 + ATTN_HEADS_PER_LOOP))
        qms, c0s = [], []
        for h in heads:
            q2 = q_ref[:, (h // 2) * LANES:(h // 2 + 1) * LANES]
            lo = (h % 2) * FOX_HEAD_DIM
            qms.append(jnp.where((lane >= lo) & (lane < lo + FOX_HEAD_DIM), q2, jnp.zeros_like(q2)))
            c0s.append(c_ref[h:h + 1, pl.ds(q0, LANES)][:, 0:1])

        def step(j, carry, mask, heads=heads, qms=qms, c0s=c0s):
            k0 = pl.multiple_of(j * tk, tk)
            keep = (rowi + q0 >= coli + k0) if mask else None
            out = []
            for n, h in enumerate(heads):
                m, lp, acc = carry[n]
                cols = slice((h // 2) * LANES, (h // 2 + 1) * LANES)
                s = _dot_nt(qms[n], k_ref[pl.ds(k0, tk), cols]) + (c0s[n] - c_ref[h:h + 1, pl.ds(k0, tk)])
                if mask:
                    s = jnp.where(keep, s, NEG_INF)
                m_new = jnp.maximum(m, jnp.max(s, axis=-1, keepdims=True))
                alpha = jnp.exp(m - m_new)
                p = jnp.exp(s - m_new)
                psum = p[:, 0:LANES]
                for t in range(1, tk // LANES):
                    psum = psum + p[:, t * LANES:(t + 1) * LANES]
                lp = alpha * lp + psum
                acc = alpha * acc + _dot(p.astype(BF16), v_ref[pl.ds(k0, tk), cols])
                out.append((m_new, lp, acc))
            return tuple(out)

        init = tuple((jnp.full((tq, 1), NEG_INF, F32), jnp.zeros((tq, LANES), F32), jnp.zeros((tq, LANES), F32))
                     for _ in heads)
        carry = lax.fori_loop(0, n_full, lambda j, cr: step(j, cr, False), init)
        res = step(n_full, carry, True)
        outs = [acc * (1.0 / jnp.sum(lp, axis=-1, keepdims=True)) for _, lp, acc in res]
        for n in range(0, len(heads), 2):
            pair = heads[n] // 2
            o_ref[:, pair * LANES:(pair + 1) * LANES] = jnp.where(lane < FOX_HEAD_DIM, outs[n], outs[n + 1]).astype(BF16)


def _attn_prompt(qb, kb, vb, c, tq, tk):
    b, s, _ = qb.shape
    return pl.pallas_call(
        functools.partial(_attn_prompt_kernel, tq=tq, tk=tk),
        grid=(b, s // tq),
        in_specs=[pl.BlockSpec((None, tq, FOX_W), lambda bi, i: (bi, i, 0)),
                  pl.BlockSpec((None, s, FOX_W), lambda bi, i: (bi, 0, 0)),
                  pl.BlockSpec((None, s, FOX_W), lambda bi, i: (bi, 0, 0)),
                  pl.BlockSpec((None, FOX_HEADS, s), lambda bi, i: (bi, 0, 0))],
        out_specs=pl.BlockSpec((None, tq, FOX_W), lambda bi, i: (bi, i, 0)),
        out_shape=jax.ShapeDtypeStruct((b, s, FOX_W), BF16),
        compiler_params=_cparams(("parallel", "arbitrary")),
        name="attn_prompt",
    )(qb, kb, vb, c)


def _attn_step_kernel(q_ref, kn_ref, vn_ref, kc_ref, vc_ref, c_ref, o_ref, *, past, t):
    lane = lax.broadcasted_iota(jnp.int32, (1, LANES), 1)
    rowi = lax.broadcasted_iota(jnp.int32, (t, t), 0)
    coli = lax.broadcasted_iota(jnp.int32, (t, t), 1)
    causal = rowi >= coli
    for pair in range(FOX_HEADS // 2):
        cols = slice(pair * LANES, (pair + 1) * LANES)
        q2 = q_ref[:, cols]
        kc = kc_ref[:, cols].astype(BF16)
        vc = vc_ref[:, cols].astype(BF16)
        kn = kn_ref[:, cols]
        vn = vn_ref[:, cols]
        outs = []
        for sub in range(2):
            h = 2 * pair + sub
            qm = jnp.where((lane >= sub * FOX_HEAD_DIM) & (lane < (sub + 1) * FOX_HEAD_DIM), q2, jnp.zeros_like(q2))
            c_c = c_ref[h:h + 1, 0:past]
            c_n = c_ref[h:h + 1, past:past + t]
            c0 = c_n[:, 0:1]
            s_c = _dot_nt(qm, kc) + (c0 - c_c)
            s_n = jnp.where(causal, _dot_nt(qm, kn) + (c0 - c_n), NEG_INF)
            m = jnp.maximum(jnp.max(s_c, axis=-1, keepdims=True), jnp.max(s_n, axis=-1, keepdims=True))
            p_c = jnp.exp(s_c - m)
            p_n = jnp.exp(s_n - m)
            l = jnp.sum(p_c, axis=-1, keepdims=True) + jnp.sum(p_n, axis=-1, keepdims=True)
            acc = _dot(p_c.astype(BF16), vc) + _dot(p_n.astype(BF16), vn)
            outs.append(acc * (1.0 / l))
        o_ref[:, cols] = jnp.where(lane < FOX_HEAD_DIM, outs[0], outs[1]).astype(BF16)


def _attn_step(qb, knb, vnb, k_cache, v_cache, c_all):
    b, t, _ = qb.shape
    past = k_cache.shape[1]
    sp = c_all.shape[-1]
    new = pl.BlockSpec((None, t, FOX_W), lambda bi: (bi, 0, 0))
    old = pl.BlockSpec((None, past, FOX_W), lambda bi: (bi, 0, 0))
    return pl.pallas_call(
        functools.partial(_attn_step_kernel, past=past, t=t),
        grid=(b,),
        in_specs=[new, new, new, old, old, pl.BlockSpec((None, FOX_HEADS, sp), lambda bi: (bi, 0, 0))],
        out_specs=new,
        out_shape=jax.ShapeDtypeStruct((b, t, FOX_W), BF16),
        compiler_params=_cparams(("parallel",)),
        name="attn_step",
    )(qb, knb, vnb, k_cache, v_cache, c_all)


def _split(a):
    hi = a.astype(BF16)
    return hi, (a - hi.astype(F32)).astype(BF16)


def _dot3(a_parts, b_parts):
    (ah, al), (bh, bl) = a_parts, b_parts
    return _dot(ah, bh) + (_dot(ah, bl) + _dot(al, bh))


def _gdn_kernel(x_ref, z_ref, sm_ref, cb_ref, s0_ref, cw_ref, nw_ref, o_ref, s_ref, xbuf, *, rows):
    c = pl.program_id(1)
    L = CHUNK

    @pl.when(c == 0)
    def _():
        xbuf[8 - (CONV_W - 1):8, :] = cb_ref[...]
        s_ref[...] = s0_ref[...]

    xbuf[8:8 + rows, :] = x_ref[...]
    conv = xbuf[5:5 + rows, :] * cw_ref[0:1, :]
    for i in range(1, CONV_W):
        conv = conv + xbuf[5 + i:5 + i + rows, :] * cw_ref[i:i + 1, :]
    xbuf[5:8, :] = xbuf[rows + 5:rows + 8, :]
    conv = conv * _sigmoid(conv)

    ri = lax.broadcasted_iota(jnp.int32, (L, L), 0)
    ci = lax.broadcasted_iota(jnp.int32, (L, L), 1)
    tri_incl = ri >= ci
    tri_strict = ri > ci
    eye = ri == ci
    eye_f = eye.astype(F32)
    tril_b = tri_incl.astype(BF16)

    units = []
    for ch in range(rows // L):
        r0 = ch * L
        sm = sm_ref[r0:r0 + L, :]
        sh, sl = _split(sm)
        sl2 = (sm - sh.astype(F32) - sl.astype(F32)).astype(BF16)
        gcum = _dot(tril_b, sh) + (_dot(tril_b, sl) + _dot(tril_b, sl2))
        for h in range(GDN_HEADS):
            xq = conv[r0:r0 + L, h * GDN_DK:(h + 1) * GDN_DK]
            xk = conv[r0:r0 + L, GDN_KW + h * GDN_DK:GDN_KW + (h + 1) * GDN_DK]
            v = conv[r0:r0 + L, 2 * GDN_KW + h * GDN_DV:2 * GDN_KW + (h + 1) * GDN_DV]
            q = xq * lax.rsqrt(jnp.sum(xq * xq, axis=-1, keepdims=True) + RMS_EPS) * (GDN_DK ** -0.5)
            k = xk * lax.rsqrt(jnp.sum(xk * xk, axis=-1, keepdims=True) + RMS_EPS)
            beta = sm[:, BETA_LO + h:BETA_LO + h + 1]
            gc = gcum[:, G_LO + h:G_LO + h + 1]
            grow = jnp.sum(jnp.where(eye, gc, 0.0), axis=0, keepdims=True)
            decay = jnp.where(tri_incl, jnp.exp(jnp.where(tri_incl, gc - grow, 0.0)), 0.0)
            eg = jnp.exp(gc)
            glast = gc[L - 1:L, :]
            kb = k.astype(BF16)
            qb = q.astype(BF16)
            a = jnp.where(tri_strict, _dot_nt(kb, kb) * decay, 0.0) * beta
            units.append(dict(r0=r0, h=h, a=a, eg=eg, eglast=jnp.exp(glast), qb=qb,
                              rhs=jnp.concatenate([beta * v, (beta * eg) * k], axis=1),
                              qk=(_dot_nt(qb, kb) * decay).astype(BF16),
                              kd=(k * jnp.exp(glast - gc)).astype(BF16)))
    xs = [eye_f - u["a"] for u in units]
    pp = [_split(u["a"]) for u in units]
    m = 2
    while m < L:
        pp = [_split(_dot3(p2, p2)) for p2 in pp]
        xs = [x + _dot3(_split(x), p2) for x, p2 in zip(xs, pp)]
        m *= 2
    for u, x in zip(units, xs):
        uw = _dot3(_split(x), _split(u["rhs"]))
        u["u"] = uw[:, :GDN_DV]
        u["qw"] = jnp.concatenate([u["qb"], uw[:, GDN_DV:].astype(BF16)], axis=0)

    state = [s_ref[h] for h in range(GDN_HEADS)]
    for u in units:
        h, r0 = u["h"], u["r0"]
        s = state[h]
        r = _dot(u["qw"], s.astype(BF16))
        db = (u["u"] - r[L:, :]).astype(BF16)
        o = u["eg"] * r[:L, :] + _dot(u["qk"], db)
        state[h] = u["eglast"] * s + _dot_tn(u["kd"], db)
        on = o * lax.rsqrt(jnp.mean(o * o, axis=-1, keepdims=True) + RMS_EPS) * nw_ref[...]
        zz = z_ref[r0:r0 + L, h * GDN_DV:(h + 1) * GDN_DV]
        o_ref[r0:r0 + L, h * GDN_DV:(h + 1) * GDN_DV] = (on * (zz * _sigmoid(zz))).astype(BF16)
    for h in range(GDN_HEADS):
        s_ref[h] = state[h]


def _gdn(gqkv, gz, small, conv_buf, s0, conv_w, norm_w, rows):
    b, t, _ = gqkv.shape
    blk = lambda w: pl.BlockSpec((None, rows, w), lambda bi, c: (bi, c, 0))
    state = pl.BlockSpec((None, GDN_HEADS, GDN_DK, GDN_DV), lambda bi, c: (bi, 0, 0, 0))
    return pl.pallas_call(
        functools.partial(_gdn_kernel, rows=rows),
        grid=(b, t // rows),
        in_specs=[blk(GDN_CONV_CH), blk(GDN_VW), blk(SMALL_W),
                  pl.BlockSpec((None, CONV_W - 1, GDN_CONV_CH), lambda bi, c: (bi, 0, 0)),
                  state,
                  pl.BlockSpec((CONV_W, GDN_CONV_CH), lambda bi, c: (0, 0)),
                  pl.BlockSpec((1, GDN_DV), lambda bi, c: (0, 0))],
        out_specs=[blk(GDN_VW), state],
        out_shape=[jax.ShapeDtypeStruct((b, t, GDN_VW), BF16),
                   jax.ShapeDtypeStruct((b, GDN_HEADS, GDN_DK, GDN_DV), F32)],
        scratch_shapes=[pltpu.VMEM((rows + 8, GDN_CONV_CH), F32)],
        compiler_params=_cparams(("parallel", "arbitrary")),
        name="gdn",
    )(gqkv, gz, small, conv_buf, s0, conv_w, norm_w)


def _post_kernel(fo_ref, go_ref, ma_ref, mb_ref, x_ref, wfo_ref, wgo_ref, wout_ref, gb_ref, nw_ref, rw_ref, rb_ref,
                 tri_ref, cin_ref, x2_ref, h2_ref, gates_ref, ir_ref, cnt_ref, carry):
    i = pl.program_id(0)

    @pl.when(i == 0)
    def _():
        carry[...] = cin_ref[...]

    ya = _dot(fo_ref[...], wfo_ref[...])
    yb = _dot(go_ref[...], wgo_ref[...])
    merged = _sigmoid(ma_ref[...] + gb_ref[0:1, :]) * ya + _sigmoid(mb_ref[...] + gb_ref[1:2, :]) * yb
    x2 = x_ref[...] + _dot(merged.astype(BF16), wout_ref[...])
    x2_ref[...] = x2
    h2 = (x2 * lax.rsqrt(jnp.mean(x2 * x2, axis=-1, keepdims=True) + RMS_EPS)) * nw_ref[...]
    h2_ref[...] = h2
    logits = _dot(h2, rw_ref[...], HIGHEST) + rb_ref[...]
    lane = lax.broadcasted_iota(jnp.int32, logits.shape, 1).astype(F32)
    work = logits
    vals, hits = [], []
    for _ in range(TOP_K):
        m = jnp.max(work, axis=-1, keepdims=True)
        idx = jnp.min(jnp.where(work == m, lane, float(LANES)), axis=-1, keepdims=True)
        hit = lane == idx
        vals.append(m)
        hits.append((hit, idx))
        work = jnp.where(hit, -jnp.inf, work)
    es = [jnp.exp(v - vals[0]) for v in vals]
    denom = es[0] + es[1] + es[2] + es[3]
    cnt = jnp.zeros(logits.shape, F32)
    for hit, _ in hits:
        cnt = cnt + hit.astype(F32)
    base = _dot(tri_ref[...], cnt.astype(BF16)) + carry[...]
    gates = jnp.zeros(logits.shape, F32)
    ir = jnp.zeros(logits.shape, F32)
    for kk, (hit, idx) in enumerate(hits):
        rank = jnp.sum(jnp.where(hit, base, 0.0), axis=-1, keepdims=True)
        gates = gates + jnp.where(lane == float(kk), es[kk] / denom, 0.0)
        ir = ir + jnp.where(lane == float(kk), idx, 0.0) + jnp.where(lane == float(TOP_K + kk), rank, 0.0)
    gates_ref[...] = gates
    ir_ref[...] = ir.astype(jnp.int32)
    carry[...] = carry[...] + jnp.sum(cnt, axis=0, keepdims=True)
    cnt_ref[...] = carry[...]


def _post(fo, go, ma, mb, x, wfo, wgo, wout, gate_bias, nw, rw, rb, cnt_in, tm):
    n = x.shape[0]
    row = lambda w: pl.BlockSpec((tm, w), lambda i: (i, 0))
    const = lambda shape: pl.BlockSpec(shape, lambda i: (0, 0))
    tri = jnp.asarray(np.tril(np.ones((tm, tm), np.float32), -1), BF16)
    return pl.pallas_call(
        _post_kernel,
        grid=(n // tm,),
        in_specs=[row(FOX_W), row(GDN_VW), row(D_MODEL), row(D_MODEL), row(D_MODEL),
                  const((FOX_W, D_MODEL)), const((GDN_VW, D_MODEL)), const((D_MODEL, D_MODEL)),
                  const((2, D_MODEL)), const((1, D_MODEL)), const((D_MODEL, LANES)), const((1, LANES)),
                  const((tm, tm)), const((1, LANES))],
        out_specs=[row(D_MODEL), row(D_MODEL), row(LANES), row(LANES), const((1, LANES))],
        out_shape=[jax.ShapeDtypeStruct((n, D_MODEL), F32), jax.ShapeDtypeStruct((n, D_MODEL), F32),
                   jax.ShapeDtypeStruct((n, LANES), F32), jax.ShapeDtypeStruct((n, LANES), jnp.int32),
                   jax.ShapeDtypeStruct((1, LANES), F32)],
        scratch_shapes=[pltpu.VMEM((1, LANES), F32)],
        compiler_params=_cparams(("arbitrary",)),
        name="post_router",
    )(fo, go, ma, mb, x, wfo, wgo, wout, gate_bias, nw, rw, rb, tri, cnt_in)


def _dispatch_kernel(dest_ref, h_ref, xs_in_ref, xs_ref, sem, *, tm):
    del xs_in_ref

    group = min(ROW_GROUP, tm)

    def issue(tg, carry):
        t0 = pl.multiple_of(tg * group, group)
        for r in range(group):
            for kk in range(TOP_K):
                d = dest_ref[(t0 + r) * TOP_K + kk]
                pltpu.make_async_copy(h_ref.at[pl.ds(t0 + r, 1), :], xs_ref.at[pl.ds(d, 1), :], sem).start()
        return carry

    lax.fori_loop(0, tm // group, issue, 0)
    for kk in range(TOP_K):
        pltpu.make_async_copy(h_ref, xs_ref.at[pl.ds(0, tm), :], sem).wait()


def _dispatch(dest_flat, h, xs, tm):
    n = h.shape[0]
    return pl.pallas_call(
        functools.partial(_dispatch_kernel, tm=tm),
        grid=(n // tm,),
        in_specs=[pl.BlockSpec((tm * TOP_K,), lambda i: (i,), memory_space=pltpu.SMEM),
                  pl.BlockSpec((tm, D_MODEL), lambda i: (i, 0)),
                  pl.BlockSpec(memory_space=pl.ANY)],
        out_specs=pl.BlockSpec(memory_space=pl.ANY),
        out_shape=jax.ShapeDtypeStruct(xs.shape, xs.dtype),
        scratch_shapes=[pltpu.SemaphoreType.DMA(())],
        input_output_aliases={2: 0},
        compiler_params=_cparams(("arbitrary",)),
        name="moe_dispatch",
    )(dest_flat, h, xs)


def _expert_kernel(be_ref, nu_ref, x_ref, wgu_ref, bgu_ref, wd_ref, bd_ref, y_ref, wgu_b, wd_b):
    i = pl.program_id(0)
    changed = jnp.logical_or(i == 0, be_ref[i] != be_ref[jnp.maximum(i - 1, 0)])

    @pl.when(changed)
    def _():
        for f in range(0, 2 * D_FF, FF_CHUNK):
            wgu_b[:, f:f + FF_CHUNK] = wgu_ref[:, f:f + FF_CHUNK].astype(BF16)
        for f in range(0, D_FF, FF_CHUNK):
            wd_b[f:f + FF_CHUNK, :] = wd_ref[f:f + FF_CHUNK, :].astype(BF16)

    @pl.when(i < nu_ref[0])
    def _():
        x = x_ref[...].astype(BF16)
        acc = jnp.zeros(y_ref.shape, F32)
        for f in range(0, D_FF, FF_CHUNK):
            gate = _dot(x, wgu_b[:, f:f + FF_CHUNK]) + bgu_ref[:, f:f + FF_CHUNK]
            up = _dot(x, wgu_b[:, D_FF + f:D_FF + f + FF_CHUNK]) + bgu_ref[:, D_FF + f:D_FF + f + FF_CHUNK]
            gate = jnp.minimum(gate, SWIGLU_LIMIT)
            up = jnp.clip(up, -SWIGLU_LIMIT, SWIGLU_LIMIT)
            act = (up + 1.0) * (gate * _sigmoid(SWIGLU_ALPHA * gate))
            acc = acc + _dot(act.astype(BF16), wd_b[f:f + FF_CHUNK, :])
        y_ref[...] = acc + bd_ref[...]

    @pl.when(i >= nu_ref[0])
    def _():
        y_ref[...] = jnp.zeros(y_ref.shape, F32)


def _experts(block_e, n_used, xs, w_gu, b_gu, w_down, b_down):
    rows = xs.shape[0]
    tb = EXPERT_BLOCK
    grid_spec = pltpu.PrefetchScalarGridSpec(
        num_scalar_prefetch=2,
        grid=(rows // tb,),
        in_specs=[pl.BlockSpec((tb, D_MODEL), lambda i, be, nu: (i, 0)),
                  pl.BlockSpec((None, D_MODEL, 2 * D_FF), lambda i, be, nu: (be[i], 0, 0)),
                  pl.BlockSpec((None, 1, 2 * D_FF), lambda i, be, nu: (be[i], 0, 0)),
                  pl.BlockSpec((None, D_FF, D_MODEL), lambda i, be, nu: (be[i], 0, 0)),
                  pl.BlockSpec((None, 1, D_MODEL), lambda i, be, nu: (be[i], 0, 0))],
        out_specs=pl.BlockSpec((tb, D_MODEL), lambda i, be, nu: (i, 0)),
        scratch_shapes=[pltpu.VMEM((D_MODEL, 2 * D_FF), BF16), pltpu.VMEM((D_FF, D_MODEL), BF16)],
    )
    return pl.pallas_call(
        _expert_kernel,
        grid_spec=grid_spec,
        out_shape=jax.ShapeDtypeStruct((rows, D_MODEL), F32),
        compiler_params=_cparams(("arbitrary",)),
        name="moe_experts",
    )(block_e, n_used, xs, w_gu, b_gu.reshape(N_EXPERTS, 1, 2 * D_FF), w_down, b_down.reshape(N_EXPERTS, 1, D_MODEL))


def _combine_kernel(dest_ref, x2_ref, gates_ref, fw_ref, ys_ref, y_ref, buf, sem, *, tm):
    group = min(ROW_GROUP, tm)

    def issue(tg, carry):
        t0 = pl.multiple_of(tg * group, group)
        for r in range(group):
            for kk in range(TOP_K):
                d = dest_ref[(t0 + r) * TOP_K + kk]
                pltpu.make_async_copy(ys_ref.at[pl.ds(d, 1), :], buf.at[kk, pl.ds(t0 + r, 1), :], sem).start()
        return carry

    lax.fori_loop(0, tm // group, issue, 0)
    for kk in range(TOP_K):
        pltpu.make_async_copy(ys_ref.at[pl.ds(0, tm), :], buf.at[kk], sem).wait()
    gates = gates_ref[...]
    out = x2_ref[...]
    for kk in range(TOP_K):
        out = out + gates[:, kk:kk + 1] * buf[kk]
    y_ref[...] = (out * lax.rsqrt(jnp.mean(out * out, axis=-1, keepdims=True) + RMS_EPS)) * fw_ref[...]


def _combine(dest_flat, x2, gates, final_w, ys, tm):
    n = x2.shape[0]
    return pl.pallas_call(
        functools.partial(_combine_kernel, tm=tm),
        grid=(n // tm,),
        in_specs=[pl.BlockSpec((tm * TOP_K,), lambda i: (i,), memory_space=pltpu.SMEM),
                  pl.BlockSpec((tm, D_MODEL), lambda i: (i, 0)),
                  pl.BlockSpec((tm, LANES), lambda i: (i, 0)),
                  pl.BlockSpec((1, D_MODEL), lambda i: (0, 0)),
                  pl.BlockSpec(memory_space=pl.ANY)],
        out_specs=pl.BlockSpec((tm, D_MODEL), lambda i: (i, 0)),
        out_shape=jax.ShapeDtypeStruct((n, D_MODEL), F32),
        scratch_shapes=[pltpu.VMEM((TOP_K, tm, D_MODEL), F32), pltpu.SemaphoreType.DMA(())],
        compiler_params=_cparams(("arbitrary",)),
        name="moe_combine",
    )(dest_flat, x2, gates, final_w, ys)


def _row_tile(n, want):
    t = min(want, n)
    while n % t:
        t //= 2
    return t


def kernel(x_prompt, x_sample, cache_fox_k, cache_fox_v, cache_fox_logf, state_gdn, state_gdn_conv, attn_norm_w, w_in, fox_f_bias, gdn_conv_w, gdn_a_log, gdn_dt_bias, gdn_norm_w, gate_bias, fox_w_o, gdn_w_o, w_out, ffn_norm_w, router_w, router_b, expert_w_gu, expert_b_gu, expert_w_down, expert_b_down, final_norm_w):
    l = 0
    bp, sp, d = x_prompt.shape
    bs, ts, _ = x_sample.shape
    past = cache_fox_k.shape[2]
    n_p, n_s = bp * sp, bs * ts

    w = w_in[l]
    o_ff = 3 * FOX_W
    o_gqkv = o_ff + FOX_HEADS
    o_gz = o_gqkv + GDN_CONV_CH
    o_ga = o_gz + GDN_VW
    o_gb = o_ga + GDN_HEADS
    o_ma = o_gb + GDN_HEADS
    w_r = jnp.concatenate(
        [w[:, :o_ff], w[:, o_gqkv:o_gz], w[:, o_gz:o_ga], w[:, o_ma:], w[:, o_ff:o_gqkv], w[:, o_ga:o_ma],
         jnp.zeros((d, SMALL_W - FOX_HEADS - 2 * GDN_HEADS), w.dtype)], axis=1).astype(BF16)
    sb = jnp.zeros((8, SMALL_W), F32)
    sb = sb.at[0, LOGF_LO:LOGF_LO + FOX_HEADS].set(fox_f_bias[l])
    sb = sb.at[0, G_LO:G_LO + GDN_HEADS].set(gdn_dt_bias[l])
    sb = sb.at[1, G_LO:G_LO + GDN_HEADS].set(gdn_a_log[l])
    anw = attn_norm_w[l].reshape(1, d)
    wfo = fox_w_o[l].astype(BF16)
    wgo = gdn_w_o[l].astype(BF16)
    wout = w_out[l].astype(BF16)
    fnw = ffn_norm_w[l].reshape(1, d)
    rw = jnp.concatenate([router_w[l], jnp.zeros((d, LANES - N_EXPERTS), F32)], axis=1)
    rb = jnp.concatenate([router_b[l], jnp.full((LANES - N_EXPERTS,), -jnp.inf, F32)]).reshape(1, LANES)
    gnw = gdn_norm_w[l].reshape(1, GDN_DV)
    final_w = final_norm_w.reshape(1, d)

    def mixer(x2d, b, t):
        q, k, v, kb, vb, gqkv, gz, ma, mb, small = _proj(x2d, anw, w_r, sb, _row_tile(x2d.shape[0], 256))
        return dict(q=q.reshape(b, t, FOX_W), k=k, v=v, kb=kb.reshape(b, t, FOX_W), vb=vb.reshape(b, t, FOX_W),
                    gqkv=gqkv.reshape(b, t, GDN_CONV_CH), gz=gz.reshape(b, t, GDN_VW), ma=ma, mb=mb,
                    small=small.reshape(b, t, SMALL_W))

    pp = mixer(x_prompt.reshape(n_p, d), bp, sp)
    logf_p = pp["small"][:, :, LOGF_LO:LOGF_LO + FOX_HEADS]
    c_p = _seq_cumsum(jnp.transpose(logf_p, (0, 2, 1)))
    fo_p = _attn_prompt(pp["q"], pp["kb"], pp["vb"], c_p, _row_tile(sp, ATTN_TQ), _row_tile(sp, ATTN_TK))
    rows_p = _row_tile(sp, 4 * CHUNK)
    go_p, state_p = _gdn(pp["gqkv"], pp["gz"], pp["small"], jnp.zeros((bp, CONV_W - 1, GDN_CONV_CH), F32),
                         jnp.zeros((bp, GDN_HEADS, GDN_DK, GDN_DV), F32), gdn_conv_w[l], gnw, rows_p)
    conv_p = pp["gqkv"][:, sp - (CONV_W - 1):, :]

    ps = mixer(x_sample.reshape(n_s, d), bs, ts)
    logf_s = ps["small"][:, :, LOGF_LO:LOGF_LO + FOX_HEADS]
    tot = past + ts
    tot_pad = -(-tot // LANES) * LANES
    logf_all = jnp.concatenate([cache_fox_logf[l].astype(F32), logf_s,
                                jnp.zeros((bs, tot_pad - tot, FOX_HEADS), F32)], axis=1)
    c_s = _seq_cumsum(jnp.transpose(logf_all, (0, 2, 1)))
    fo_s = _attn_step(ps["q"], ps["kb"], ps["vb"], cache_fox_k[l].reshape(bs, past, FOX_W),
                      cache_fox_v[l].reshape(bs, past, FOX_W), c_s)
    t_pad = -(-ts // CHUNK) * CHUNK
    padt = lambda a: jnp.pad(a, ((0, 0), (0, t_pad - ts), (0, 0)))
    go_s, state_s = _gdn(padt(ps["gqkv"]), padt(ps["gz"]), padt(ps["small"]), state_gdn_conv[l], state_gdn[l],
                         gdn_conv_w[l], gnw, CHUNK)
    go_s = go_s[:, :ts, :]
    conv_s = ps["gqkv"][:, ts - (CONV_W - 1):, :]

    tm_p = _row_tile(n_p, 256)
    tm_s = _row_tile(n_s, 256)
    x2_p, h2_p, gates_p, ir_p, cnt_p = _post(fo_p.reshape(n_p, FOX_W), go_p.reshape(n_p, GDN_VW), pp["ma"], pp["mb"],
                                             x_prompt.reshape(n_p, d), wfo, wgo, wout, gate_bias[l], fnw, rw, rb,
                                             jnp.zeros((1, LANES), F32), tm_p)
    x2_s, h2_s, gates_s, ir_s, cnt_s = _post(fo_s.reshape(n_s, FOX_W), go_s.reshape(n_s, GDN_VW), ps["ma"], ps["mb"],
                                             x_sample.reshape(n_s, d), wfo, wgo, wout, gate_bias[l], fnw, rw, rb,
                                             cnt_p, tm_s)

    tb = EXPERT_BLOCK
    nk = (n_p + n_s) * TOP_K
    n_blocks = -(-nk // tb) + N_EXPERTS
    counts = cnt_s[0, :N_EXPERTS].astype(jnp.int32)
    padded = (counts + tb - 1) // tb * tb
    pad_end = jnp.cumsum(padded)
    pad_start = pad_end - padded
    block_e = jnp.minimum(jnp.searchsorted(pad_end, jnp.arange(n_blocks, dtype=jnp.int32) * tb, side='right'),
                          N_EXPERTS - 1).astype(jnp.int32)
    n_used = (pad_end[-1:] // tb).astype(jnp.int32)
    dest = lambda ir: (pad_start[ir[:, :TOP_K]] + ir[:, TOP_K:2 * TOP_K]).reshape(-1)
    dest_p, dest_s = dest(ir_p), dest(ir_s)

    xs = jnp.zeros((n_blocks * tb, d), F32)
    xs = _dispatch(dest_p, h2_p, xs, tm_p)
    xs = _dispatch(dest_s, h2_s, xs, tm_s)
    ys = _experts(block_e, n_used, xs, expert_w_gu[l], expert_b_gu[l], expert_w_down[l], expert_b_down[l])
    y_p = _combine(dest_p, x2_p, gates_p, final_w, ys, tm_p)
    y_s = _combine(dest_s, x2_s, gates_s, final_w, ys, tm_s)

    hd = (FOX_HEADS, FOX_HEAD_DIM)
    return (y_p.reshape(bp, sp, d), y_s.reshape(bs, ts, d),
            pp["k"].reshape(1, bp, sp, *hd), pp["v"].reshape(1, bp, sp, *hd), logf_p[None],
            state_p[None], conv_p[None],
            ps["k"].reshape(1, bs, ts, *hd), ps["v"].reshape(1, bs, ts, *hd), logf_s[None],
            state_s[None], conv_s[None])
```

```python
import functools

import numpy as np
import jax
import jax.numpy as jnp
from jax import lax
from jax.experimental import pallas as pl
from jax.experimental.pallas import tpu as pltpu

F32 = jnp.float32
BF16 = jnp.bfloat16
HIGHEST = lax.Precision.HIGHEST

D_MODEL = 1024
FOX_HEADS = 8
FOX_HEAD_DIM = 64
FOX_W = FOX_HEADS * FOX_HEAD_DIM
FOX_SCALE = FOX_HEAD_DIM ** -0.5
GDN_HEADS = 4
GDN_DK = 128
GDN_DV = 128
GDN_KW = GDN_HEADS * GDN_DK
GDN_VW = GDN_HEADS * GDN_DV
GDN_CONV_CH = 2 * GDN_KW + GDN_VW
CONV_W = 4
CHUNK = 64
N_EXPERTS = 32
TOP_K = 4
D_FF = D_MODEL
SWIGLU_LIMIT = 7.0
SWIGLU_ALPHA = 1.702
RMS_EPS = 1e-6
NEG_INF = -1e30

LANES = 128
SMALL_W = LANES
LOGF_LO, G_LO, BETA_LO = 0, FOX_HEADS, FOX_HEADS + GDN_HEADS
C_Q, C_K, C_V = 0, FOX_W, 2 * FOX_W
C_GQKV = 3 * FOX_W
C_GZ = C_GQKV + GDN_CONV_CH
C_MA = C_GZ + GDN_VW
C_MB = C_MA + D_MODEL
C_SMALL = C_MB + D_MODEL
W_R = C_SMALL + SMALL_W

VMEM_LIMIT = 56 * 1024 * 1024
ATTN_TQ = 512
ATTN_TK = 512
ATTN_HEADS_PER_LOOP = 2
ROW_GROUP = 8
EXPERT_BLOCK = 512
PAD_GROUP_BITS = (EXPERT_BLOCK // ROW_GROUP).bit_length() - 1
ZERO_ROWS = ROW_GROUP << (PAD_GROUP_BITS - 1)
FF_CHUNK = 512


def _cparams(sem, vmem=VMEM_LIMIT):
    return pltpu.CompilerParams(dimension_semantics=sem, vmem_limit_bytes=vmem)


def _softplus(x):
    return jnp.maximum(x, 0.0) + jnp.log1p(jnp.exp(-jnp.abs(x)))


def _sigmoid(x):
    return jax.nn.sigmoid(x)


def _dot(a, b, precision=None):
    return jnp.dot(a, b, preferred_element_type=F32, precision=precision)


def _dot_nt(a, b, precision=None):
    return lax.dot_general(a, b, (((1,), (1,)), ((), ())), preferred_element_type=F32, precision=precision)


def _dot_tn(a, b, precision=None):
    return lax.dot_general(a, b, (((0,), (0,)), ((), ())), preferred_element_type=F32, precision=precision)


def _proj_kernel(x_ref, nw_ref, w_ref, sb_ref, q_ref, k_ref, v_ref, kb_ref, vb_ref, gqkv_ref, gz_ref, ma_ref,
                 mb_ref, small_ref):
    x = x_ref[...]
    ms = jnp.mean(x * x, axis=-1, keepdims=True)
    h = ((x * lax.rsqrt(ms + RMS_EPS)) * nw_ref[...]).astype(BF16)

    def sec(lo, width):
        return _dot(h, w_ref[:, lo:lo + width])

    q_ref[...] = (sec(C_Q, FOX_W) * FOX_SCALE).astype(BF16)
    k = sec(C_K, FOX_W)
    k_ref[...] = k
    kb_ref[...] = k.astype(BF16)
    v = sec(C_V, FOX_W)
    v_ref[...] = v
    vb_ref[...] = v.astype(BF16)
    gqkv_ref[...] = sec(C_GQKV, GDN_CONV_CH)
    gz_ref[...] = sec(C_GZ, GDN_VW)
    ma_ref[...] = sec(C_MA, D_MODEL)
    mb_ref[...] = sec(C_MB, D_MODEL)
    z = sec(C_SMALL, SMALL_W) + sb_ref[0:1, :]
    lane = lax.broadcasted_iota(jnp.int32, z.shape, 1)
    logf = -_softplus(-z)
    g = -jnp.exp(sb_ref[1:2, :]) * _softplus(z)
    beta = _sigmoid(z)
    small_ref[...] = jnp.where(lane < G_LO, logf, jnp.where(lane < BETA_LO, g, beta))


def _proj(x2d, nw, w_r, sb, tm):
    n = x2d.shape[0]
    row = lambda w: pl.BlockSpec((tm, w), lambda i: (i, 0))
    const = lambda shape: pl.BlockSpec(shape, lambda i: (0, 0))
    outs = [(FOX_W, BF16), (FOX_W, F32), (FOX_W, F32), (FOX_W, BF16), (FOX_W, BF16), (GDN_CONV_CH, F32),
            (GDN_VW, F32), (D_MODEL, F32), (D_MODEL, F32), (SMALL_W, F32)]
    return pl.pallas_call(
        _proj_kernel,
        grid=(n // tm,),
        in_specs=[row(D_MODEL), const((1, D_MODEL)), const((D_MODEL, W_R)), const((8, SMALL_W))],
        out_specs=[row(w) for w, _ in outs],
        out_shape=[jax.ShapeDtypeStruct((n, w), dt) for w, dt in outs],
        compiler_params=_cparams(("parallel",)),
        name="proj",
    )(x2d, nw, w_r, sb)


def _cumsum_kernel(x_ref, tri_ref, lmat_ref, c_ref):
    x = x_ref[...]
    cb = _dot(x, tri_ref[...], HIGHEST)
    tot = jnp.broadcast_to(cb[:, LANES - 1:LANES], cb.shape)
    c_ref[...] = cb + _dot(lmat_ref[...], tot, HIGHEST)


def _seq_cumsum(logf_bhs):
    b, h, s = logf_bhs.shape
    nb = s // LANES
    r = h * nb
    tri = jnp.asarray(np.triu(np.ones((LANES, LANES), np.float32)))
    rr = np.arange(r)
    lmat = jnp.asarray(((rr[:, None] // nb == rr[None, :] // nb) & (rr[None, :] < rr[:, None])).astype(np.float32))
    out = pl.pallas_call(
        _cumsum_kernel,
        grid=(b,),
        in_specs=[pl.BlockSpec((None, r, LANES), lambda i: (i, 0, 0)),
                  pl.BlockSpec((LANES, LANES), lambda i: (0, 0)),
                  pl.BlockSpec((r, r), lambda i: (0, 0))],
        out_specs=pl.BlockSpec((None, r, LANES), lambda i: (i, 0, 0)),
        out_shape=jax.ShapeDtypeStruct((b, r, LANES), F32),
        compiler_params=_cparams(("parallel",)),
        name="logf_cumsum",
    )(logf_bhs.reshape(b, r, LANES), tri, lmat)
    return out.reshape(b, h, s)


def _attn_prompt_kernel(q_ref, k_ref, v_ref, c_ref, o_ref, *, tq, tk):
    i = pl.program_id(1)
    q0 = pl.multiple_of(i * tq, tq)
    n_full = (i * tq) // tk
    lane = lax.broadcasted_iota(jnp.int32, (1, LANES), 1)
    rowi = lax.broadcasted_iota(jnp.int32, (tq, tk), 0)
    coli = lax.broadcasted_iota(jnp.int32, (tq, tk), 1)
    for g0 in range(0, FOX_HEADS, ATTN_HEADS_PER_LOOP):
        heads = list(range(g0, g0 + ATTN_HEADS_PER_LOOP))
        qms, c0s = [], []
        for h in heads:
            q2 = q_ref[:, (h // 2) * LANES:(h // 2 + 1) * LANES]
            lo = (h % 2) * FOX_HEAD_DIM
            qms.append(jnp.where((lane >= lo) & (lane < lo + FOX_HEAD_DIM), q2, jnp.zeros_like(q2)))
            c0s.append(c_ref[h:h + 1, pl.ds(q0, LANES)][:, 0:1])

        def step(j, carry, mask, heads=heads, qms=qms, c0s=c0s):
            k0 = pl.multiple_of(j * tk, tk)
            keep = (rowi + q0 >= coli + k0) if mask else None
            out = []
            for n, h in enumerate(heads):
                m, lp, acc = carry[n]
                cols = slice((h // 2) * LANES, (h // 2 + 1) * LANES)
                s = _dot_nt(qms[n], k_ref[pl.ds(k0, tk), cols]) + (c0s[n] - c_ref[h:h + 1, pl.ds(k0, tk)])
                if mask:
                    s = jnp.where(keep, s, NEG_INF)
                m_new = jnp.maximum(m, jnp.max(s, axis=-1, keepdims=True))
                alpha = jnp.exp(m - m_new)
                p = jnp.exp(s - m_new)
                psum = p[:, 0:LANES]
                for t in range(1, tk // LANES):
                    psum = psum + p[:, t * LANES:(t + 1) * LANES]
                lp = alpha * lp + psum
                acc = alpha * acc + _dot(p.astype(BF16), v_ref[pl.ds(k0, tk), cols])
                out.append((m_new, lp, acc))
            return tuple(out)

        init = tuple((jnp.full((tq, 1), NEG_INF, F32), jnp.zeros((tq, LANES), F32), jnp.zeros((tq, LANES), F32))
                     for _ in heads)
        carry = lax.fori_loop(0, n_full, lambda j, cr: step(j, cr, False), init)
        res = step(n_full, carry, True)
        outs = [acc * (1.0 / jnp.sum(lp, axis=-1, keepdims=True)) for _, lp, acc in res]
        for n in range(0, len(heads), 2):
            pair = heads[n] // 2
            o_ref[:, pair * LANES:(pair + 1) * LANES] = jnp.where(lane < FOX_HEAD_DIM, outs[n], outs[n + 1]).astype(BF16)


def _attn_prompt(qb, kb, vb, c, tq, tk):
    b, s, _ = qb.shape
    return pl.pallas_call(
        functools.partial(_attn_prompt_kernel, tq=tq, tk=tk),
        grid=(b, s // tq),
        in_specs=[pl.BlockSpec((None, tq, FOX_W), lambda bi, i: (bi, i, 0)),
                  pl.BlockSpec((None, s, FOX_W), lambda bi, i: (bi, 0, 0)),
                  pl.BlockSpec((None, s, FOX_W), lambda bi, i: (bi, 0, 0)),
                  pl.BlockSpec((None, FOX_HEADS, s), lambda bi, i: (bi, 0, 0))],
        out_specs=pl.BlockSpec((None, tq, FOX_W), lambda bi, i: (bi, i, 0)),
        out_shape=jax.ShapeDtypeStruct((b, s, FOX_W), BF16),
        compiler_params=_cparams(("parallel", "arbitrary")),
        name="attn_prompt",
    )(qb, kb, vb, c)


def _attn_step_kernel(q_ref, kn_ref, vn_ref, kc_ref, vc_ref, c_ref, o_ref, *, past, t):
    lane = lax.broadcasted_iota(jnp.int32, (1, LANES), 1)
    rowi = lax.broadcasted_iota(jnp.int32, (t, t), 0)
    coli = lax.broadcasted_iota(jnp.int32, (t, t), 1)
    causal = rowi >= coli
    for pair in range(FOX_HEADS // 2):
        cols = slice(pair * LANES, (pair + 1) * LANES)
        q2 = q_ref[:, cols]
        kc = kc_ref[:, cols].astype(BF16)
        vc = vc_ref[:, cols].astype(BF16)
        kn = kn_ref[:, cols]
        vn = vn_ref[:, cols]
        outs = []
        for sub in range(2):
            h = 2 * pair + sub
            qm = jnp.where((lane >= sub * FOX_HEAD_DIM) & (lane < (sub + 1) * FOX_HEAD_DIM), q2, jnp.zeros_like(q2))
            c_c = c_ref[h:h + 1, 0:past]
            c_n = c_ref[h:h + 1, past:past + t]
            c0 = c_n[:, 0:1]
            s_c = _dot_nt(qm, kc) + (c0 - c_c)
            s_n = jnp.where(causal, _dot_nt(qm, kn) + (c0 - c_n), NEG_INF)
            m = jnp.maximum(jnp.max(s_c, axis=-1, keepdims=True), jnp.max(s_n, axis=-1, keepdims=True))
            p_c = jnp.exp(s_c - m)
            p_n = jnp.exp(s_n - m)
            l = jnp.sum(p_c, axis=-1, keepdims=True) + jnp.sum(p_n, axis=-1, keepdims=True)
            acc = _dot(p_c.astype(BF16), vc) + _dot(p_n.astype(BF16), vn)
            outs.append(acc * (1.0 / l))
        o_ref[:, cols] = jnp.where(lane < FOX_HEAD_DIM, outs[0], outs[1]).astype(BF16)


def _attn_step(qb, knb, vnb, k_cache, v_cache, c_all):
    b, t, _ = qb.shape
    past = k_cache.shape[1]
    sp = c_all.shape[-1]
    new = pl.BlockSpec((None, t, FOX_W), lambda bi: (bi, 0, 0))
    old = pl.BlockSpec((None, past, FOX_W), lambda bi: (bi, 0, 0))
    return pl.pallas_call(
        functools.partial(_attn_step_kernel, past=past, t=t),
        grid=(b,),
        in_specs=[new, new, new, old, old, pl.BlockSpec((None, FOX_HEADS, sp), lambda bi: (bi, 0, 0))],
        out_specs=new,
        out_shape=jax.ShapeDtypeStruct((b, t, FOX_W), BF16),
        compiler_params=_cparams(("parallel",)),
        name="attn_step",
    )(qb, knb, vnb, k_cache, v_cache, c_all)


def _split(a):
    hi = a.astype(BF16)
    return hi, (a - hi.astype(F32)).astype(BF16)


def _dot3(a_parts, b_parts):
    (ah, al), (bh, bl) = a_parts, b_parts
    return _dot(ah, bh) + (_dot(ah, bl) + _dot(al, bh))


def _gdn_kernel(x_ref, z_ref, sm_ref, cb_ref, s0_ref, cw_ref, nw_ref, o_ref, s_ref, xbuf, *, rows):
    c = pl.program_id(1)
    L = CHUNK

    @pl.when(c == 0)
    def _():
        xbuf[8 - (CONV_W - 1):8, :] = cb_ref[...]
        s_ref[...] = s0_ref[...]

    xbuf[8:8 + rows, :] = x_ref[...]
    conv = xbuf[5:5 + rows, :] * cw_ref[0:1, :]
    for i in range(1, CONV_W):
        conv = conv + xbuf[5 + i:5 + i + rows, :] * cw_ref[i:i + 1, :]
    xbuf[5:8, :] = xbuf[rows + 5:rows + 8, :]
    conv = conv * _sigmoid(conv)

    ri = lax.broadcasted_iota(jnp.int32, (L, L), 0)
    ci = lax.broadcasted_iota(jnp.int32, (L, L), 1)
    tri_incl = ri >= ci
    tri_strict = ri > ci
    eye = ri == ci
    eye_f = eye.astype(F32)
    tril_b = tri_incl.astype(BF16)

    units = []
    for ch in range(rows // L):
        r0 = ch * L
        sm = sm_ref[r0:r0 + L, :]
        sh, sl = _split(sm)
        sl2 = (sm - sh.astype(F32) - sl.astype(F32)).astype(BF16)
        gcum = _dot(tril_b, sh) + (_dot(tril_b, sl) + _dot(tril_b, sl2))
        for h in range(GDN_HEADS):
            xq = conv[r0:r0 + L, h * GDN_DK:(h + 1) * GDN_DK]
            xk = conv[r0:r0 + L, GDN_KW + h * GDN_DK:GDN_KW + (h + 1) * GDN_DK]
            v = conv[r0:r0 + L, 2 * GDN_KW + h * GDN_DV:2 * GDN_KW + (h + 1) * GDN_DV]
            q = xq * lax.rsqrt(jnp.sum(xq * xq, axis=-1, keepdims=True) + RMS_EPS) * (GDN_DK ** -0.5)
            k = xk * lax.rsqrt(jnp.sum(xk * xk, axis=-1, keepdims=True) + RMS_EPS)
            beta = sm[:, BETA_LO + h:BETA_LO + h + 1]
            gc = gcum[:, G_LO + h:G_LO + h + 1]
            grow = jnp.sum(jnp.where(eye, gc, 0.0), axis=0, keepdims=True)
            decay = jnp.where(tri_incl, jnp.exp(jnp.where(tri_incl, gc - grow, 0.0)), 0.0)
            eg = jnp.exp(gc)
            glast = gc[L - 1:L, :]
            kb = k.astype(BF16)
            qb = q.astype(BF16)
            a = jnp.where(tri_strict, _dot_nt(kb, kb) * decay, 0.0) * beta
            units.append(dict(r0=r0, h=h, a=a, eg=eg, eglast=jnp.exp(glast), qb=qb,
                              rhs=jnp.concatenate([beta * v, (beta * eg) * k], axis=1),
                              qk=(_dot_nt(qb, kb) * decay).astype(BF16),
                              kd=(k * jnp.exp(glast - gc)).astype(BF16)))
    xs = [eye_f - u["a"] for u in units]
    pp = [_split(u["a"]) for u in units]
    m = 2
    while m < L:
        pp = [_split(_dot3(p2, p2)) for p2 in pp]
        xs = [x + _dot3(_split(x), p2) for x, p2 in zip(xs, pp)]
        m *= 2
    for u, x in zip(units, xs):
        uw = _dot3(_split(x), _split(u["rhs"]))
        u["u"] = uw[:, :GDN_DV]
        u["qw"] = jnp.concatenate([u["qb"], uw[:, GDN_DV:].astype(BF16)], axis=0)

    state = [s_ref[h] for h in range(GDN_HEADS)]
    for u in units:
        h, r0 = u["h"], u["r0"]
        s = state[h]
        r = _dot(u["qw"], s.astype(BF16))
        db = (u["u"] - r[L:, :]).astype(BF16)
        o = u["eg"] * r[:L, :] + _dot(u["qk"], db)
        state[h] = u["eglast"] * s + _dot_tn(u["kd"], db)
        on = o * lax.rsqrt(jnp.mean(o * o, axis=-1, keepdims=True) + RMS_EPS) * nw_ref[...]
        zz = z_ref[r0:r0 + L, h * GDN_DV:(h + 1) * GDN_DV]
        o_ref[r0:r0 + L, h * GDN_DV:(h + 1) * GDN_DV] = (on * (zz * _sigmoid(zz))).astype(BF16)
    for h in range(GDN_HEADS):
        s_ref[h] = state[h]


def _gdn(gqkv, gz, small, conv_buf, s0, conv_w, norm_w, rows):
    b, t, _ = gqkv.shape
    blk = lambda w: pl.BlockSpec((None, rows, w), lambda bi, c: (bi, c, 0))
    state = pl.BlockSpec((None, GDN_HEADS, GDN_DK, GDN_DV), lambda bi, c: (bi, 0, 0, 0))
    return pl.pallas_call(
        functools.partial(_gdn_kernel, rows=rows),
        grid=(b, t // rows),
        in_specs=[blk(GDN_CONV_CH), blk(GDN_VW), blk(SMALL_W),
                  pl.BlockSpec((None, CONV_W - 1, GDN_CONV_CH), lambda bi, c: (bi, 0, 0)),
                  state,
                  pl.BlockSpec((CONV_W, GDN_CONV_CH), lambda bi, c: (0, 0)),
                  pl.BlockSpec((1, GDN_DV), lambda bi, c: (0, 0))],
        out_specs=[blk(GDN_VW), state],
        out_shape=[jax.ShapeDtypeStruct((b, t, GDN_VW), BF16),
                   jax.ShapeDtypeStruct((b, GDN_HEADS, GDN_DK, GDN_DV), F32)],
        scratch_shapes=[pltpu.VMEM((rows + 8, GDN_CONV_CH), F32)],
        compiler_params=_cparams(("parallel", "arbitrary")),
        name="gdn",
    )(gqkv, gz, small, conv_buf, s0, conv_w, norm_w)


def _post_kernel(fo_ref, go_ref, ma_ref, mb_ref, x_ref, wfo_ref, wgo_ref, wout_ref, gb_ref, nw_ref, rw_ref, rb_ref,
                 tri_ref, cin_ref, x2_ref, h2_ref, gates_ref, ir_ref, cnt_ref, carry):
    i = pl.program_id(0)

    @pl.when(i == 0)
    def _():
        carry[...] = cin_ref[...]

    ya = _dot(fo_ref[...], wfo_ref[...])
    yb = _dot(go_ref[...], wgo_ref[...])
    merged = _sigmoid(ma_ref[...] + gb_ref[0:1, :]) * ya + _sigmoid(mb_ref[...] + gb_ref[1:2, :]) * yb
    x2 = x_ref[...] + _dot(merged.astype(BF16), wout_ref[...])
    x2_ref[...] = x2
    h2 = (x2 * lax.rsqrt(jnp.mean(x2 * x2, axis=-1, keepdims=True) + RMS_EPS)) * nw_ref[...]
    h2_ref[...] = h2
    logits = _dot3(_split(h2), (rw_ref[0], rw_ref[1])) + rb_ref[...]
    lane = lax.broadcasted_iota(jnp.int32, logits.shape, 1).astype(F32)
    work = logits
    vals, hits = [], []
    for _ in range(TOP_K):
        m = jnp.max(work, axis=-1, keepdims=True)
        idx = jnp.min(jnp.where(work == m, lane, float(LANES)), axis=-1, keepdims=True)
        hit = lane == idx
        vals.append(m)
        hits.append((hit, idx))
        work = jnp.where(hit, -jnp.inf, work)
    es = [jnp.exp(v - vals[0]) for v in vals]
    denom = es[0] + es[1] + es[2] + es[3]
    cnt = jnp.zeros(logits.shape, F32)
    for hit, _ in hits:
        cnt = cnt + hit.astype(F32)
    base = _dot(tri_ref[...], cnt.astype(BF16)) + carry[...]
    gates = jnp.zeros(logits.shape, F32)
    ir = jnp.zeros(logits.shape, F32)
    for kk, (hit, idx) in enumerate(hits):
        rank = jnp.sum(jnp.where(hit, base, 0.0), axis=-1, keepdims=True)
        gates = gates + jnp.where(lane == float(kk), es[kk] / denom, 0.0)
        ir = ir + jnp.where(lane == float(kk), idx, 0.0) + jnp.where(lane == float(TOP_K + kk), rank, 0.0)
    gates_ref[...] = gates
    ir_ref[...] = ir.astype(jnp.int32)
    carry[...] = carry[...] + jnp.sum(cnt, axis=0, keepdims=True)
    cnt_ref[...] = carry[...]


def _post(fo, go, ma, mb, x, wfo, wgo, wout, gate_bias, nw, rw, rb, cnt_in, tm):
    n = x.shape[0]
    row = lambda w: pl.BlockSpec((tm, w), lambda i: (i, 0))
    const = lambda shape: pl.BlockSpec(shape, lambda i: (0, 0))
    tri = jnp.asarray(np.tril(np.ones((tm, tm), np.float32), -1), BF16)
    return pl.pallas_call(
        _post_kernel,
        grid=(n // tm,),
        in_specs=[row(FOX_W), row(GDN_VW), row(D_MODEL), row(D_MODEL), row(D_MODEL),
                  const((FOX_W, D_MODEL)), const((GDN_VW, D_MODEL)), const((D_MODEL, D_MODEL)),
                  const((2, D_MODEL)), const((1, D_MODEL)),
                  pl.BlockSpec((2, D_MODEL, LANES), lambda i: (0, 0, 0)), const((1, LANES)),
                  const((tm, tm)), const((1, LANES))],
        out_specs=[row(D_MODEL), row(D_MODEL), row(LANES), row(LANES), const((1, LANES))],
        out_shape=[jax.ShapeDtypeStruct((n, D_MODEL), F32), jax.ShapeDtypeStruct((n, D_MODEL), F32),
                   jax.ShapeDtypeStruct((n, LANES), F32), jax.ShapeDtypeStruct((n, LANES), jnp.int32),
                   jax.ShapeDtypeStruct((1, LANES), F32)],
        scratch_shapes=[pltpu.VMEM((1, LANES), F32)],
        compiler_params=_cparams(("arbitrary",)),
        name="post_router",
    )(fo, go, ma, mb, x, wfo, wgo, wout, gate_bias, nw, rw, rb, tri, cnt_in)


def _dispatch_kernel(pad_ref, dest_ref, hp_ref, hs_ref, xs_ref, sem, zbuf, zsem, *, tm, np_tiles):
    i = pl.program_id(0)
    zrows = zbuf.shape[0]

    @pl.when(i == 0)
    def _():
        zbuf[...] = jnp.zeros(zbuf.shape, zbuf.dtype)
        for phase in ("start", "wait"):
            def run(cp, phase=phase):
                cp.start() if phase == "start" else cp.wait()

            def per_expert(e, carry, run=run):
                first = pad_ref[e]
                pos = pad_ref[2 * N_EXPERTS + e]
                groups = pad_ref[3 * N_EXPERTS + e]

                def one(j, c2):
                    run(pltpu.make_async_copy(zbuf.at[pl.ds(0, 1), :], xs_ref.at[pl.ds(first + j, 1), :], zsem))
                    return c2

                lax.fori_loop(0, pad_ref[N_EXPERTS + e], one, 0)
                for b in reversed(range(PAD_GROUP_BITS)):
                    size = ROW_GROUP << b
                    bit = (groups >> b) & 1

                    @pl.when(bit == 1)
                    def _(pos=pos, size=size):
                        run(pltpu.make_async_copy(zbuf.at[pl.ds(0, size), :],
                                                  xs_ref.at[pl.ds(pl.multiple_of(pos, ROW_GROUP), size), :], zsem))

                    pos = pos + bit * size
                return carry

            lax.fori_loop(0, N_EXPERTS, per_expert, 0)

            def tail(j, carry, run=run):
                row = pl.multiple_of(pad_ref[4 * N_EXPERTS] + j * zrows, zrows)
                run(pltpu.make_async_copy(zbuf, xs_ref.at[pl.ds(row, zrows), :], zsem))
                return carry

            lax.fori_loop(0, pad_ref[4 * N_EXPERTS + 1], tail, 0)

    group = min(ROW_GROUP, tm)

    def scatter(h_ref):
        def issue(tg, carry):
            t0 = pl.multiple_of(tg * group, group)
            for r in range(group):
                for kk in range(TOP_K):
                    d = dest_ref[(t0 + r) * TOP_K + kk]
                    pltpu.make_async_copy(h_ref.at[pl.ds(t0 + r, 1), :], xs_ref.at[pl.ds(d, 1), :],
                                          sem).start(priority=kk % 2)
            return carry

        lax.fori_loop(0, tm // group, issue, 0)
        for kk in range(TOP_K):
            pltpu.make_async_copy(h_ref, xs_ref.at[pl.ds(0, tm), :], sem).wait()

    @pl.when(i < np_tiles)
    def _():
        scatter(hp_ref)

    @pl.when(i >= np_tiles)
    def _():
        scatter(hs_ref)


def _dispatch(dest_flat, h_p, h_s, pad_tab, rows, tm):
    np_tiles, ns_tiles = h_p.shape[0] // tm, h_s.shape[0] // tm
    return pl.pallas_call(
        functools.partial(_dispatch_kernel, tm=tm, np_tiles=np_tiles),
        grid_spec=pltpu.PrefetchScalarGridSpec(
            num_scalar_prefetch=1, grid=(np_tiles + ns_tiles,),
            in_specs=[pl.BlockSpec((tm * TOP_K,), lambda i, pad: (i,), memory_space=pltpu.SMEM),
                      pl.BlockSpec((tm, D_MODEL), lambda i, pad: (jnp.minimum(i, np_tiles - 1), 0)),
                      pl.BlockSpec((tm, D_MODEL), lambda i, pad: (jnp.maximum(i - np_tiles, 0), 0))],
            out_specs=pl.BlockSpec(memory_space=pl.ANY),
            scratch_shapes=[pltpu.SemaphoreType.DMA(()), pltpu.VMEM((ZERO_ROWS, D_MODEL), F32),
                            pltpu.SemaphoreType.DMA(())]),
        out_shape=jax.ShapeDtypeStruct((rows, D_MODEL), F32),
        compiler_params=_cparams(("arbitrary",)),
        name="moe_dispatch",
    )(pad_tab, dest_flat, h_p, h_s)


def _expert_kernel(be_ref, nu_ref, x_ref, wgu_ref, bgu_ref, wd_ref, bd_ref, y_ref, wgu_b, wd_b):
    i = pl.program_id(0)
    changed = jnp.logical_or(i == 0, be_ref[i] != be_ref[jnp.maximum(i - 1, 0)])

    @pl.when(changed)
    def _():
        for f in range(0, 2 * D_FF, FF_CHUNK):
            wgu_b[:, f:f + FF_CHUNK] = wgu_ref[:, f:f + FF_CHUNK].astype(BF16)
        for f in range(0, D_FF, FF_CHUNK):
            wd_b[f:f + FF_CHUNK, :] = wd_ref[f:f + FF_CHUNK, :].astype(BF16)

    @pl.when(i < nu_ref[0])
    def _():
        x = x_ref[...].astype(BF16)
        acc = jnp.zeros(y_ref.shape, F32)
        for f in range(0, D_FF, FF_CHUNK):
            gate = _dot(x, wgu_b[:, f:f + FF_CHUNK]) + bgu_ref[:, f:f + FF_CHUNK]
            up = _dot(x, wgu_b[:, D_FF + f:D_FF + f + FF_CHUNK]) + bgu_ref[:, D_FF + f:D_FF + f + FF_CHUNK]
            gate = jnp.minimum(gate, SWIGLU_LIMIT)
            up = jnp.clip(up, -SWIGLU_LIMIT, SWIGLU_LIMIT)
            act = (up + 1.0) * (gate * _sigmoid(SWIGLU_ALPHA * gate))
            acc = acc + _dot(act.astype(BF16), wd_b[f:f + FF_CHUNK, :])
        y_ref[...] = acc + bd_ref[...]

    @pl.when(i >= nu_ref[0])
    def _():
        y_ref[...] = jnp.zeros(y_ref.shape, F32)


def _experts(block_e, n_used, xs, w_gu, b_gu, w_down, b_down):
    rows = xs.shape[0]
    tb = EXPERT_BLOCK
    grid_spec = pltpu.PrefetchScalarGridSpec(
        num_scalar_prefetch=2,
        grid=(rows // tb,),
        in_specs=[pl.BlockSpec((tb, D_MODEL), lambda i, be, nu: (jnp.minimum(i, jnp.maximum(nu[0] - 1, 0)), 0)),
                  pl.BlockSpec((None, D_MODEL, 2 * D_FF), lambda i, be, nu: (be[i], 0, 0)),
                  pl.BlockSpec((None, 1, 2 * D_FF), lambda i, be, nu: (be[i], 0, 0)),
                  pl.BlockSpec((None, D_FF, D_MODEL), lambda i, be, nu: (be[i], 0, 0)),
                  pl.BlockSpec((None, 1, D_MODEL), lambda i, be, nu: (be[i], 0, 0))],
        out_specs=pl.BlockSpec((tb, D_MODEL), lambda i, be, nu: (i, 0)),
        scratch_shapes=[pltpu.VMEM((D_MODEL, 2 * D_FF), BF16), pltpu.VMEM((D_FF, D_MODEL), BF16)],
    )
    return pl.pallas_call(
        _expert_kernel,
        grid_spec=grid_spec,
        out_shape=jax.ShapeDtypeStruct((rows, D_MODEL), F32),
        compiler_params=_cparams(("arbitrary",)),
        name="moe_experts",
    )(block_e, n_used, xs, w_gu, b_gu.reshape(N_EXPERTS, 1, 2 * D_FF), w_down, b_down.reshape(N_EXPERTS, 1, D_MODEL))


def _combine_kernel(dest_ref, x2_ref, gates_ref, fw_ref, ys_ref, y_ref, buf, sem, *, tm):
    group = min(ROW_GROUP, tm)

    def issue(tg, carry):
        t0 = pl.multiple_of(tg * group, group)
        for r in range(group):
            for kk in range(TOP_K):
                d = dest_ref[(t0 + r) * TOP_K + kk]
                pltpu.make_async_copy(ys_ref.at[pl.ds(d, 1), :], buf.at[kk, pl.ds(t0 + r, 1), :], sem).start(priority=kk % 2)
        return carry

    lax.fori_loop(0, tm // group, issue, 0)
    for kk in range(TOP_K):
        pltpu.make_async_copy(ys_ref.at[pl.ds(0, tm), :], buf.at[kk], sem).wait()
    gates = gates_ref[...]
    out = x2_ref[...]
    for kk in range(TOP_K):
        out = out + gates[:, kk:kk + 1] * buf[kk]
    y_ref[...] = (out * lax.rsqrt(jnp.mean(out * out, axis=-1, keepdims=True) + RMS_EPS)) * fw_ref[...]


def _combine(dest_flat, x2, gates, final_w, ys, tm):
    n = x2.shape[0]
    return pl.pallas_call(
        functools.partial(_combine_kernel, tm=tm),
        grid=(n // tm,),
        in_specs=[pl.BlockSpec((tm * TOP_K,), lambda i: (i,), memory_space=pltpu.SMEM),
                  pl.BlockSpec((tm, D_MODEL), lambda i: (i, 0)),
                  pl.BlockSpec((tm, LANES), lambda i: (i, 0)),
                  pl.BlockSpec((1, D_MODEL), lambda i: (0, 0)),
                  pl.BlockSpec(memory_space=pl.ANY)],
        out_specs=pl.BlockSpec((tm, D_MODEL), lambda i: (i, 0)),
        out_shape=jax.ShapeDtypeStruct((n, D_MODEL), F32),
        scratch_shapes=[pltpu.VMEM((TOP_K, tm, D_MODEL), F32), pltpu.SemaphoreType.DMA(())],
        compiler_params=_cparams(("arbitrary",)),
        name="moe_combine",
    )(dest_flat, x2, gates, final_w, ys)


def _row_tile(n, want):
    t = min(want, n)
    while n % t:
        t //= 2
    return t


def kernel(x_prompt, x_sample, cache_fox_k, cache_fox_v, cache_fox_logf, state_gdn, state_gdn_conv, attn_norm_w, w_in, fox_f_bias, gdn_conv_w, gdn_a_log, gdn_dt_bias, gdn_norm_w, gate_bias, fox_w_o, gdn_w_o, w_out, ffn_norm_w, router_w, router_b, expert_w_gu, expert_b_gu, expert_w_down, expert_b_down, final_norm_w):
    l = 0
    bp, sp, d = x_prompt.shape
    bs, ts, _ = x_sample.shape
    past = cache_fox_k.shape[2]
    n_p, n_s = bp * sp, bs * ts

    w = w_in[l]
    o_ff = 3 * FOX_W
    o_gqkv = o_ff + FOX_HEADS
    o_gz = o_gqkv + GDN_CONV_CH
    o_ga = o_gz + GDN_VW
    o_gb = o_ga + GDN_HEADS
    o_ma = o_gb + GDN_HEADS
    w_r = jnp.concatenate(
        [w[:, :o_ff], w[:, o_gqkv:o_gz], w[:, o_gz:o_ga], w[:, o_ma:], w[:, o_ff:o_gqkv], w[:, o_ga:o_ma],
         jnp.zeros((d, SMALL_W - FOX_HEADS - 2 * GDN_HEADS), w.dtype)], axis=1).astype(BF16)
    sb = jnp.zeros((8, SMALL_W), F32)
    sb = sb.at[0, LOGF_LO:LOGF_LO + FOX_HEADS].set(fox_f_bias[l])
    sb = sb.at[0, G_LO:G_LO + GDN_HEADS].set(gdn_dt_bias[l])
    sb = sb.at[1, G_LO:G_LO + GDN_HEADS].set(gdn_a_log[l])
    anw = attn_norm_w[l].reshape(1, d)
    wfo = fox_w_o[l].astype(BF16)
    wgo = gdn_w_o[l].astype(BF16)
    wout = w_out[l].astype(BF16)
    fnw = ffn_norm_w[l].reshape(1, d)
    rw = jnp.concatenate([router_w[l], jnp.zeros((d, LANES - N_EXPERTS), F32)], axis=1)
    rw_hi = rw.astype(BF16)
    rw = jnp.stack([rw_hi, (rw - rw_hi.astype(F32)).astype(BF16)])
    rb = jnp.concatenate([router_b[l], jnp.full((LANES - N_EXPERTS,), -jnp.inf, F32)]).reshape(1, LANES)
    gnw = gdn_norm_w[l].reshape(1, GDN_DV)
    final_w = final_norm_w.reshape(1, d)

    def mixer(x2d, b, t):
        q, k, v, kb, vb, gqkv, gz, ma, mb, small = _proj(x2d, anw, w_r, sb, _row_tile(x2d.shape[0], 256))
        return dict(q=q.reshape(b, t, FOX_W), k=k, v=v, kb=kb.reshape(b, t, FOX_W), vb=vb.reshape(b, t, FOX_W),
                    gqkv=gqkv.reshape(b, t, GDN_CONV_CH), gz=gz.reshape(b, t, GDN_VW), ma=ma, mb=mb,
                    small=small.reshape(b, t, SMALL_W))

    pp = mixer(x_prompt.reshape(n_p, d), bp, sp)
    logf_p = pp["small"][:, :, LOGF_LO:LOGF_LO + FOX_HEADS]
    c_p = _seq_cumsum(jnp.transpose(logf_p, (0, 2, 1)))
    fo_p = _attn_prompt(pp["q"], pp["kb"], pp["vb"], c_p, _row_tile(sp, ATTN_TQ), _row_tile(sp, ATTN_TK))
    rows_p = _row_tile(sp, 4 * CHUNK)
    go_p, state_p = _gdn(pp["gqkv"], pp["gz"], pp["small"], jnp.zeros((bp, CONV_W - 1, GDN_CONV_CH), F32),
                         jnp.zeros((bp, GDN_HEADS, GDN_DK, GDN_DV), F32), gdn_conv_w[l], gnw, rows_p)
    conv_p = pp["gqkv"][:, sp - (CONV_W - 1):, :]

    ps = mixer(x_sample.reshape(n_s, d), bs, ts)
    logf_s = ps["small"][:, :, LOGF_LO:LOGF_LO + FOX_HEADS]
    tot = past + ts
    tot_pad = -(-tot // LANES) * LANES
    logf_all = jnp.concatenate([cache_fox_logf[l].astype(F32), logf_s,
                                jnp.zeros((bs, tot_pad - tot, FOX_HEADS), F32)], axis=1)
    c_s = _seq_cumsum(jnp.transpose(logf_all, (0, 2, 1)))
    fo_s = _attn_step(ps["q"], ps["kb"], ps["vb"], cache_fox_k[l].reshape(bs, past, FOX_W),
                      cache_fox_v[l].reshape(bs, past, FOX_W), c_s)
    t_pad = -(-ts // CHUNK) * CHUNK
    padt = lambda a: jnp.pad(a, ((0, 0), (0, t_pad - ts), (0, 0)))
    go_s, state_s = _gdn(padt(ps["gqkv"]), padt(ps["gz"]), padt(ps["small"]), state_gdn_conv[l], state_gdn[l],
                         gdn_conv_w[l], gnw, CHUNK)
    go_s = go_s[:, :ts, :]
    conv_s = ps["gqkv"][:, ts - (CONV_W - 1):, :]

    tm_p = _row_tile(n_p, 512)
    tm_s = _row_tile(n_s, 512)
    x2_p, h2_p, gates_p, ir_p, cnt_p = _post(fo_p.reshape(n_p, FOX_W), go_p.reshape(n_p, GDN_VW), pp["ma"], pp["mb"],
                                             x_prompt.reshape(n_p, d), wfo, wgo, wout, gate_bias[l], fnw, rw, rb,
                                             jnp.zeros((1, LANES), F32), tm_p)
    x2_s, h2_s, gates_s, ir_s, cnt_s = _post(fo_s.reshape(n_s, FOX_W), go_s.reshape(n_s, GDN_VW), ps["ma"], ps["mb"],
                                             x_sample.reshape(n_s, d), wfo, wgo, wout, gate_bias[l], fnw, rw, rb,
                                             cnt_p, tm_s)

    tb = EXPERT_BLOCK
    nk = (n_p + n_s) * TOP_K
    n_blocks = -(-nk // tb) + N_EXPERTS
    counts = cnt_s[0, :N_EXPERTS].astype(jnp.int32)
    padded = (counts + tb - 1) // tb * tb
    pad_end = jnp.cumsum(padded)
    pad_start = pad_end - padded
    block_pos = jnp.arange(n_blocks, dtype=jnp.int32) * tb
    block_e = jnp.minimum(jnp.sum((pad_end[None, :] <= block_pos[:, None]).astype(jnp.int32), axis=1), N_EXPERTS - 1)
    n_used = (pad_end[-1:] // tb).astype(jnp.int32)
    dest = lambda ir: (pad_start[ir[:, :TOP_K]] + ir[:, TOP_K:2 * TOP_K]).reshape(-1)
    dest_p, dest_s = dest(ir_p), dest(ir_s)

    pad_first = pad_start + counts
    pad_aligned = (pad_first + ROW_GROUP - 1) // ROW_GROUP * ROW_GROUP
    rows = n_blocks * tb
    pad_tab = jnp.concatenate([pad_first, pad_aligned - pad_first, pad_aligned, (pad_end - pad_aligned) // ROW_GROUP,
                               pad_end[-1:], (rows - pad_end[-1:]) // ZERO_ROWS]).astype(jnp.int32)
    tm_d = _row_tile(n_s, 512)
    assert n_p % tm_d == 0 and tb % ZERO_ROWS == 0
    xs = _dispatch(jnp.concatenate([dest_p, dest_s]), h2_p, h2_s, pad_tab, rows, tm_d)
    ys = _experts(block_e, n_used, xs, expert_w_gu[l], expert_b_gu[l], expert_w_down[l], expert_b_down[l])
    y_p = _combine(dest_p, x2_p, gates_p, final_w, ys, tm_p)
    y_s = _combine(dest_s, x2_s, gates_s, final_w, ys, tm_s)

    hd = (FOX_HEADS, FOX_HEAD_DIM)
    return (y_p.reshape(bp, sp, d), y_s.reshape(bs, ts, d),
            pp["k"].reshape(1, bp, sp, *hd), pp["v"].reshape(1, bp, sp, *hd), logf_p[None],
            state_p[None], conv_p[None],
            ps["k"].reshape(1, bs, ts, *hd), ps["v"].reshape(1, bs, ts, *hd), logf_s[None],
            state_s[None], conv_s[None])
```

```python
import functools

import numpy as np
import jax
import jax.numpy as jnp
from jax import lax
from jax.experimental import pallas as pl
from jax.experimental.pallas import tpu as pltpu

F32 = jnp.float32
BF16 = jnp.bfloat16
HIGHEST = lax.Precision.HIGHEST

D_MODEL = 1024
FOX_HEADS = 8
FOX_HEAD_DIM = 64
FOX_W = FOX_HEADS * FOX_HEAD_DIM
FOX_SCALE = FOX_HEAD_DIM ** -0.5
GDN_HEADS = 4
GDN_DK = 128
GDN_DV = 128
GDN_KW = GDN_HEADS * GDN_DK
GDN_VW = GDN_HEADS * GDN_DV
GDN_CONV_CH = 2 * GDN_KW + GDN_VW
CONV_W = 4
CHUNK = 64
N_EXPERTS = 32
TOP_K = 4
D_FF = D_MODEL
SWIGLU_LIMIT = 7.0
SWIGLU_ALPHA = 1.702
RMS_EPS = 1e-6
NEG_INF = -1e30

LANES = 128
SMALL_W = LANES
LOGF_LO, G_LO, BETA_LO = 0, FOX_HEADS, FOX_HEADS + GDN_HEADS
C_Q, C_K, C_V = 0, FOX_W, 2 * FOX_W
C_GQKV = 3 * FOX_W
C_GZ = C_GQKV + GDN_CONV_CH
C_MA = C_GZ + GDN_VW
C_MB = C_MA + D_MODEL
C_SMALL = C_MB + D_MODEL
W_R = C_SMALL + SMALL_W

VMEM_LIMIT = 56 * 1024 * 1024
ATTN_TQ = 512
ATTN_TK = 512
ATTN_HEADS_PER_LOOP = 4
ROW_GROUP = 8
DMA_ROWS_PER_TRIP = 8
EXPERT_BLOCK = 512
PAD_GROUP_BITS = (EXPERT_BLOCK // ROW_GROUP).bit_length() - 1
ZERO_ROWS = ROW_GROUP << (PAD_GROUP_BITS - 1)
FF_CHUNK = 512


def _cparams(sem, vmem=VMEM_LIMIT):
    return pltpu.CompilerParams(dimension_semantics=sem, vmem_limit_bytes=vmem)


def _softplus(x):
    return jnp.maximum(x, 0.0) + jnp.log1p(jnp.exp(-jnp.abs(x)))


def _sigmoid(x):
    return jax.nn.sigmoid(x)


def _dot(a, b, precision=None):
    return jnp.dot(a, b, preferred_element_type=F32, precision=precision)


def _dot_nt(a, b, precision=None):
    return lax.dot_general(a, b, (((1,), (1,)), ((), ())), preferred_element_type=F32, precision=precision)


def _dot_tn(a, b, precision=None):
    return lax.dot_general(a, b, (((0,), (0,)), ((), ())), preferred_element_type=F32, precision=precision)


def _head_tiles(x):
    tiles = []
    for pair in range(FOX_HEADS // 2):
        t = x[:, pair * LANES:(pair + 1) * LANES]
        tiles += [t, pltpu.roll(t, FOX_HEAD_DIM, axis=1)]
    return tiles


def _proj_kernel(x_ref, nw_ref, w_ref, sb_ref, *rest, aug, tiles_per_seq):
    if aug:
        tri_ref, place_ref, q_ref, k_ref, v_ref, kb_ref, vb_ref, gqkv_ref, gz_ref, ma_ref, mb_ref, small_ref, carry = rest
    else:
        q_ref, k_ref, v_ref, kb_ref, vb_ref, gqkv_ref, gz_ref, ma_ref, mb_ref, small_ref = rest
    x = x_ref[...]
    ms = jnp.mean(x * x, axis=-1, keepdims=True)
    h = ((x * lax.rsqrt(ms + RMS_EPS)) * nw_ref[...]).astype(BF16)

    def sec(lo, width):
        return _dot(h, w_ref[:, lo:lo + width])

    q = sec(C_Q, FOX_W) * FOX_SCALE
    k = sec(C_K, FOX_W)
    k_ref[...] = k
    v = sec(C_V, FOX_W)
    v_ref[...] = v
    gqkv_ref[...] = sec(C_GQKV, GDN_CONV_CH)
    gz_ref[...] = sec(C_GZ, GDN_VW)
    ma_ref[...] = sec(C_MA, D_MODEL)
    mb_ref[...] = sec(C_MB, D_MODEL)
    z = sec(C_SMALL, SMALL_W) + sb_ref[0:1, :]
    lane = lax.broadcasted_iota(jnp.int32, z.shape, 1)
    logf = -_softplus(-z)
    g = -jnp.exp(sb_ref[1:2, :]) * _softplus(z)
    beta = _sigmoid(z)
    small_ref[...] = jnp.where(lane < G_LO, logf, jnp.where(lane < BETA_LO, g, beta))
    if not aug:
        q_ref[...] = q.astype(BF16)
        kb_ref[...] = k.astype(BF16)
        vb_ref[...] = v.astype(BF16)
        return

    @pl.when(pl.program_id(0) % tiles_per_seq == 0)
    def _():
        carry[...] = jnp.zeros(carry.shape, F32)

    lf = jnp.where(lane < G_LO, logf, 0.0)
    l1 = lf.astype(BF16)
    r1 = lf - l1.astype(F32)
    l2 = r1.astype(BF16)
    l3 = (r1 - l2.astype(F32)).astype(BF16)
    tri = tri_ref[...]
    c = (_dot(tri, l1) + (_dot(tri, l2) + _dot(tri, l3))) + carry[...]
    carry[...] = c[c.shape[0] - 1:, :]
    nc = -c
    p1 = nc.astype(BF16).astype(F32)
    r1 = nc - p1
    p2 = r1.astype(BF16).astype(F32)
    p3 = (r1 - p2).astype(BF16).astype(F32)
    parts = p1 + pltpu.roll(p2, FOX_HEADS, axis=1) + pltpu.roll(p3, 2 * FOX_HEADS, axis=1)
    ext = _dot(parts.astype(BF16), place_ref[...])
    low = lane < FOX_HEAD_DIM
    ones3 = jnp.where((lane >= FOX_HEAD_DIM) & (lane < FOX_HEAD_DIM + 3), 1.0, 0.0)
    one1 = jnp.where(lane == FOX_HEAD_DIM, 1.0, 0.0)
    for hd, (qt, kt, vt) in enumerate(zip(_head_tiles(q), _head_tiles(k), _head_tiles(v))):
        cols = slice(hd * LANES, (hd + 1) * LANES)
        q_ref[:, cols] = jnp.where(low, qt, ones3).astype(BF16)
        kb_ref[:, cols] = jnp.where(low, kt, ext[:, cols]).astype(BF16)
        vb_ref[:, cols] = jnp.where(low, vt, one1).astype(BF16)


def _proj(x2d, nw, w_r, sb, tm, seq=None):
    n = x2d.shape[0]
    aug = seq is not None
    row = lambda w: pl.BlockSpec((tm, w), lambda i: (i, 0))
    const = lambda shape: pl.BlockSpec(shape, lambda i: (0, 0))
    aw = FOX_HEADS * LANES if aug else FOX_W
    outs = [(aw, BF16), (FOX_W, F32), (FOX_W, F32), (aw, BF16), (aw, BF16), (GDN_CONV_CH, F32),
            (GDN_VW, F32), (D_MODEL, F32), (D_MODEL, F32), (SMALL_W, F32)]
    in_specs = [row(D_MODEL), const((1, D_MODEL)), const((D_MODEL, W_R)), const((8, SMALL_W))]
    args = [x2d, nw, w_r, sb]
    scratch = []
    if aug:
        place = np.zeros((LANES, FOX_HEADS * LANES), np.float32)
        for hd in range(FOX_HEADS):
            for j in range(3):
                place[j * FOX_HEADS + hd, hd * LANES + FOX_HEAD_DIM + j] = 1.0
        in_specs += [const((tm, tm)), const((LANES, FOX_HEADS * LANES))]
        args += [jnp.asarray(np.tril(np.ones((tm, tm), np.float32)), BF16), jnp.asarray(place, BF16)]
        scratch = [pltpu.VMEM((1, SMALL_W), F32)]
    return pl.pallas_call(
        functools.partial(_proj_kernel, aug=aug, tiles_per_seq=(seq // tm if aug else 1)),
        grid=(n // tm,),
        in_specs=in_specs,
        out_specs=[row(w) for w, _ in outs],
        out_shape=[jax.ShapeDtypeStruct((n, w), dt) for w, dt in outs],
        scratch_shapes=scratch,
        compiler_params=_cparams(("arbitrary",)),
        name="proj",
    )(*args)


def _cumsum_kernel(x_ref, tri_ref, lmat_ref, c_ref):
    x = x_ref[...]
    cb = _dot(x, tri_ref[...], HIGHEST)
    tot = jnp.broadcast_to(cb[:, LANES - 1:LANES], cb.shape)
    c_ref[...] = cb + _dot(lmat_ref[...], tot, HIGHEST)


def _seq_cumsum(logf_bhs):
    b, h, s = logf_bhs.shape
    nb = s // LANES
    r = h * nb
    tri = jnp.asarray(np.triu(np.ones((LANES, LANES), np.float32)))
    rr = np.arange(r)
    lmat = jnp.asarray(((rr[:, None] // nb == rr[None, :] // nb) & (rr[None, :] < rr[:, None])).astype(np.float32))
    out = pl.pallas_call(
        _cumsum_kernel,
        grid=(b,),
        in_specs=[pl.BlockSpec((None, r, LANES), lambda i: (i, 0, 0)),
                  pl.BlockSpec((LANES, LANES), lambda i: (0, 0)),
                  pl.BlockSpec((r, r), lambda i: (0, 0))],
        out_specs=pl.BlockSpec((None, r, LANES), lambda i: (i, 0, 0)),
        out_shape=jax.ShapeDtypeStruct((b, r, LANES), F32),
        compiler_params=_cparams(("parallel",)),
        name="logf_cumsum",
    )(logf_bhs.reshape(b, r, LANES), tri, lmat)
    return out.reshape(b, h, s)


def _attn_prompt_kernel(q_ref, k_ref, v_ref, o_ref, *, tq, tk):
    i = pl.program_id(1)
    q0 = pl.multiple_of(i * tq, tq)
    n_full = (i * tq) // tk
    lane = lax.broadcasted_iota(jnp.int32, (1, LANES), 1)
    rowi = lax.broadcasted_iota(jnp.int32, (tq, tk), 0)
    coli = lax.broadcasted_iota(jnp.int32, (tq, tk), 1)
    outs = []
    for g0 in range(0, FOX_HEADS, ATTN_HEADS_PER_LOOP):
        heads = list(range(g0, g0 + ATTN_HEADS_PER_LOOP))

        def step(j, carry, mask, heads=heads):
            k0 = pl.multiple_of(j * tk, tk)
            keep = (rowi + q0 >= coli + k0) if mask else None
            ss = [_dot_nt(q_ref[:, h * LANES:(h + 1) * LANES], k_ref[pl.ds(k0, tk), h * LANES:(h + 1) * LANES])
                  for h in heads]
            out = []
            for (m, acc), s, h in zip(carry, ss, heads):
                if mask:
                    s = jnp.where(keep, s, NEG_INF)
                m_new = jnp.maximum(m, jnp.max(s, axis=-1, keepdims=True))
                p = jnp.exp(s - m_new).astype(BF16)
                acc = jnp.exp(m - m_new) * acc + _dot(p, v_ref[pl.ds(k0, tk), h * LANES:(h + 1) * LANES])
                out.append((m_new, acc))
            return tuple(out)

        init = tuple((jnp.full((tq, 1), NEG_INF, F32), jnp.zeros((tq, LANES), F32)) for _ in heads)
        carry = lax.fori_loop(0, n_full, lambda j, cr, step=step: step(j, cr, False), init)
        for _, acc in step(n_full, carry, True):
            outs.append(acc * (1.0 / acc[:, FOX_HEAD_DIM:FOX_HEAD_DIM + 1]))
    for pair in range(FOX_HEADS // 2):
        o_ref[:, pair * LANES:(pair + 1) * LANES] = jnp.where(
            lane < FOX_HEAD_DIM, outs[2 * pair], pltpu.roll(outs[2 * pair + 1], FOX_HEAD_DIM, axis=1)).astype(BF16)


def _attn_prompt(qa, ka, va, tq, tk):
    b, s, aw = qa.shape
    return pl.pallas_call(
        functools.partial(_attn_prompt_kernel, tq=tq, tk=tk),
        grid=(b, s // tq),
        in_specs=[pl.BlockSpec((None, tq, aw), lambda bi, i: (bi, i, 0)),
                  pl.BlockSpec((None, s, aw), lambda bi, i: (bi, 0, 0)),
                  pl.BlockSpec((None, s, aw), lambda bi, i: (bi, 0, 0))],
        out_specs=pl.BlockSpec((None, tq, FOX_W), lambda bi, i: (bi, i, 0)),
        out_shape=jax.ShapeDtypeStruct((b, s, FOX_W), BF16),
        compiler_params=_cparams(("parallel", "arbitrary")),
        name="attn_prompt",
    )(qa, ka, va)


def _attn_step_kernel(q_ref, kn_ref, vn_ref, kc_ref, vc_ref, c_ref, o_ref, *, past, t):
    lane = lax.broadcasted_iota(jnp.int32, (1, LANES), 1)
    rowi = lax.broadcasted_iota(jnp.int32, (t, t), 0)
    coli = lax.broadcasted_iota(jnp.int32, (t, t), 1)
    causal = rowi >= coli
    for pair in range(FOX_HEADS // 2):
        cols = slice(pair * LANES, (pair + 1) * LANES)
        q2 = q_ref[:, cols]
        kc = kc_ref[:, cols].astype(BF16)
        vc = vc_ref[:, cols].astype(BF16)
        kn = kn_ref[:, cols]
        vn = vn_ref[:, cols]
        outs = []
        for sub in range(2):
            h = 2 * pair + sub
            qm = jnp.where((lane >= sub * FOX_HEAD_DIM) & (lane < (sub + 1) * FOX_HEAD_DIM), q2, jnp.zeros_like(q2))
            c_c = c_ref[h:h + 1, 0:past]
            c_n = c_ref[h:h + 1, past:past + t]
            c0 = c_n[:, 0:1]
            s_c = _dot_nt(qm, kc) + (c0 - c_c)
            s_n = jnp.where(causal, _dot_nt(qm, kn) + (c0 - c_n), NEG_INF)
            m = jnp.maximum(jnp.max(s_c, axis=-1, keepdims=True), jnp.max(s_n, axis=-1, keepdims=True))
            p_c = jnp.exp(s_c - m)
            p_n = jnp.exp(s_n - m)
            l = jnp.sum(p_c, axis=-1, keepdims=True) + jnp.sum(p_n, axis=-1, keepdims=True)
            acc = _dot(p_c.astype(BF16), vc) + _dot(p_n.astype(BF16), vn)
            outs.append(acc * (1.0 / l))
        o_ref[:, cols] = jnp.where(lane < FOX_HEAD_DIM, outs[0], outs[1]).astype(BF16)


def _attn_step(qb, knb, vnb, k_cache, v_cache, c_all):
    b, t, _ = qb.shape
    past = k_cache.shape[1]
    sp = c_all.shape[-1]
    new = pl.BlockSpec((None, t, FOX_W), lambda bi: (bi, 0, 0))
    old = pl.BlockSpec((None, past, FOX_W), lambda bi: (bi, 0, 0))
    return pl.pallas_call(
        functools.partial(_attn_step_kernel, past=past, t=t),
        grid=(b,),
        in_specs=[new, new, new, old, old, pl.BlockSpec((None, FOX_HEADS, sp), lambda bi: (bi, 0, 0))],
        out_specs=new,
        out_shape=jax.ShapeDtypeStruct((b, t, FOX_W), BF16),
        compiler_params=_cparams(("parallel",)),
        name="attn_step",
    )(qb, knb, vnb, k_cache, v_cache, c_all)


def _split(a):
    hi = a.astype(BF16)
    return hi, (a - hi.astype(F32)).astype(BF16)


def _dot3(a_parts, b_parts):
    (ah, al), (bh, bl) = a_parts, b_parts
    return _dot(ah, bh) + (_dot(ah, bl) + _dot(al, bh))


def _gdn_kernel(x_ref, z_ref, sm_ref, cb_ref, s0_ref, cw_ref, nw_ref, o_ref, s_ref, xbuf, *, rows):
    c = pl.program_id(1)
    L = CHUNK

    @pl.when(c == 0)
    def _():
        xbuf[8 - (CONV_W - 1):8, :] = cb_ref[...]
        s_ref[...] = s0_ref[...]

    xbuf[8:8 + rows, :] = x_ref[...]
    conv = xbuf[5:5 + rows, :] * cw_ref[0:1, :]
    for i in range(1, CONV_W):
        conv = conv + xbuf[5 + i:5 + i + rows, :] * cw_ref[i:i + 1, :]
    xbuf[5:8, :] = xbuf[rows + 5:rows + 8, :]
    conv = conv * _sigmoid(conv)

    ri = lax.broadcasted_iota(jnp.int32, (L, L), 0)
    ci = lax.broadcasted_iota(jnp.int32, (L, L), 1)
    tri_incl = ri >= ci
    tri_strict = ri > ci
    eye = ri == ci
    eye_f = eye.astype(F32)
    tril_b = tri_incl.astype(BF16)

    units = []
    for ch in range(rows // L):
        r0 = ch * L
        sm = sm_ref[r0:r0 + L, :]
        sh, sl = _split(sm)
        sl2 = (sm - sh.astype(F32) - sl.astype(F32)).astype(BF16)
        gcum = _dot(tril_b, sh) + (_dot(tril_b, sl) + _dot(tril_b, sl2))
        for h in range(GDN_HEADS):
            xq = conv[r0:r0 + L, h * GDN_DK:(h + 1) * GDN_DK]
            xk = conv[r0:r0 + L, GDN_KW + h * GDN_DK:GDN_KW + (h + 1) * GDN_DK]
            v = conv[r0:r0 + L, 2 * GDN_KW + h * GDN_DV:2 * GDN_KW + (h + 1) * GDN_DV]
            q = xq * lax.rsqrt(jnp.sum(xq * xq, axis=-1, keepdims=True) + RMS_EPS) * (GDN_DK ** -0.5)
            k = xk * lax.rsqrt(jnp.sum(xk * xk, axis=-1, keepdims=True) + RMS_EPS)
            beta = sm[:, BETA_LO + h:BETA_LO + h + 1]
            gc = gcum[:, G_LO + h:G_LO + h + 1]
            grow = jnp.sum(jnp.where(eye, gc, 0.0), axis=0, keepdims=True)
            decay = jnp.where(tri_incl, jnp.exp(jnp.where(tri_incl, gc - grow, 0.0)), 0.0)
            eg = jnp.exp(gc)
            glast = gc[L - 1:L, :]
            kb = k.astype(BF16)
            qb = q.astype(BF16)
            a = jnp.where(tri_strict, _dot_nt(kb, kb) * decay, 0.0) * beta
            units.append(dict(r0=r0, h=h, a=a, eg=eg, eglast=jnp.exp(glast), qb=qb,
                              rhs=jnp.concatenate([beta * v, (beta * eg) * k], axis=1),
                              qk=(_dot_nt(qb, kb) * decay).astype(BF16),
                              kd=(k * jnp.exp(glast - gc)).astype(BF16)))
    xs = [eye_f - u["a"] for u in units]
    pp = [_split(u["a"]) for u in units]
    m = 2
    while m < L:
        pp = [_split(_dot3(p2, p2)) for p2 in pp]
        xs = [x + _dot3(_split(x), p2) for x, p2 in zip(xs, pp)]
        m *= 2
    for u, x in zip(units, xs):
        uw = _dot3(_split(x), _split(u["rhs"]))
        u["u"] = uw[:, :GDN_DV]
        u["qw"] = jnp.concatenate([u["qb"], uw[:, GDN_DV:].astype(BF16)], axis=0)

    state = [s_ref[h] for h in range(GDN_HEADS)]
    for u in units:
        h, r0 = u["h"], u["r0"]
        s = state[h]
        r = _dot(u["qw"], s.astype(BF16))
        db = (u["u"] - r[L:, :]).astype(BF16)
        o = u["eg"] * r[:L, :] + _dot(u["qk"], db)
        state[h] = u["eglast"] * s + _dot_tn(u["kd"], db)
        on = o * lax.rsqrt(jnp.mean(o * o, axis=-1, keepdims=True) + RMS_EPS) * nw_ref[...]
        zz = z_ref[r0:r0 + L, h * GDN_DV:(h + 1) * GDN_DV]
        o_ref[r0:r0 + L, h * GDN_DV:(h + 1) * GDN_DV] = (on * (zz * _sigmoid(zz))).astype(BF16)
    for h in range(GDN_HEADS):
        s_ref[h] = state[h]


def _gdn(gqkv, gz, small, conv_buf, s0, conv_w, norm_w, rows):
    b, t, _ = gqkv.shape
    blk = lambda w: pl.BlockSpec((None, rows, w), lambda bi, c: (bi, c, 0))
    state = pl.BlockSpec((None, GDN_HEADS, GDN_DK, GDN_DV), lambda bi, c: (bi, 0, 0, 0))
    return pl.pallas_call(
        functools.partial(_gdn_kernel, rows=rows),
        grid=(b, t // rows),
        in_specs=[blk(GDN_CONV_CH), blk(GDN_VW), blk(SMALL_W),
                  pl.BlockSpec((None, CONV_W - 1, GDN_CONV_CH), lambda bi, c: (bi, 0, 0)),
                  state,
                  pl.BlockSpec((CONV_W, GDN_CONV_CH), lambda bi, c: (0, 0)),
                  pl.BlockSpec((1, GDN_DV), lambda bi, c: (0, 0))],
        out_specs=[blk(GDN_VW), state],
        out_shape=[jax.ShapeDtypeStruct((b, t, GDN_VW), BF16),
                   jax.ShapeDtypeStruct((b, GDN_HEADS, GDN_DK, GDN_DV), F32)],
        scratch_shapes=[pltpu.VMEM((rows + 8, GDN_CONV_CH), F32)],
        compiler_params=_cparams(("parallel", "arbitrary")),
        name="gdn",
    )(gqkv, gz, small, conv_buf, s0, conv_w, norm_w)


def _post_kernel(fo_ref, go_ref, ma_ref, mb_ref, x_ref, wfo_ref, wgo_ref, wout_ref, gb_ref, nw_ref, rw_ref, rb_ref,
                 tri_ref, cin_ref, x2_ref, h2_ref, gates_ref, ir_ref, cnt_ref, carry):
    i = pl.program_id(0)

    @pl.when(i == 0)
    def _():
        carry[...] = cin_ref[...]

    ya = _dot(fo_ref[...], wfo_ref[...])
    yb = _dot(go_ref[...], wgo_ref[...])
    merged = _sigmoid(ma_ref[...] + gb_ref[0:1, :]) * ya + _sigmoid(mb_ref[...] + gb_ref[1:2, :]) * yb
    x2 = x_ref[...] + _dot(merged.astype(BF16), wout_ref[...])
    x2_ref[...] = x2
    h2 = (x2 * lax.rsqrt(jnp.mean(x2 * x2, axis=-1, keepdims=True) + RMS_EPS)) * nw_ref[...]
    h2_ref[...] = h2
    logits = _dot3(_split(h2), (rw_ref[0], rw_ref[1])) + rb_ref[...]
    lane = lax.broadcasted_iota(jnp.int32, logits.shape, 1).astype(F32)
    work = logits
    vals, hits = [], []
    for _ in range(TOP_K):
        m = jnp.max(work, axis=-1, keepdims=True)
        idx = jnp.min(jnp.where(work == m, lane, float(LANES)), axis=-1, keepdims=True)
        hit = lane == idx
        vals.append(m)
        hits.append((hit, idx))
        work = jnp.where(hit, -jnp.inf, work)
    es = [jnp.exp(v - vals[0]) for v in vals]
    denom = es[0] + es[1] + es[2] + es[3]
    cnt = jnp.zeros(logits.shape, F32)
    for hit, _ in hits:
        cnt = cnt + hit.astype(F32)
    base = _dot(tri_ref[...], cnt.astype(BF16)) + carry[...]
    gates = jnp.zeros(logits.shape, F32)
    ir = jnp.zeros(logits.shape, F32)
    for kk, (hit, idx) in enumerate(hits):
        rank = jnp.sum(jnp.where(hit, base, 0.0), axis=-1, keepdims=True)
        gates = gates + jnp.where(lane == float(kk), es[kk] / denom, 0.0)
        ir = ir + jnp.where(lane == float(kk), idx, 0.0) + jnp.where(lane == float(TOP_K + kk), rank, 0.0)
    gates_ref[...] = gates
    ir_ref[...] = jnp.transpose(ir)[0:2 * TOP_K, :].astype(jnp.int32)
    carry[...] = carry[...] + jnp.sum(cnt, axis=0, keepdims=True)
    cnt_ref[...] = carry[...]


def _post(fo, go, ma, mb, x, wfo, wgo, wout, gate_bias, nw, rw, rb, cnt_in, tm):
    n = x.shape[0]
    row = lambda w: pl.BlockSpec((tm, w), lambda i: (i, 0))
    const = lambda shape: pl.BlockSpec(shape, lambda i: (0, 0))
    tri = jnp.asarray(np.tril(np.ones((tm, tm), np.float32), -1), BF16)
    return pl.pallas_call(
        _post_kernel,
        grid=(n // tm,),
        in_specs=[row(FOX_W), row(GDN_VW), row(D_MODEL), row(D_MODEL), row(D_MODEL),
                  const((FOX_W, D_MODEL)), const((GDN_VW, D_MODEL)), const((D_MODEL, D_MODEL)),
                  const((2, D_MODEL)), const((1, D_MODEL)),
                  pl.BlockSpec((2, D_MODEL, LANES), lambda i: (0, 0, 0)), const((1, LANES)),
                  const((tm, tm)), const((1, LANES))],
        out_specs=[row(D_MODEL), row(D_MODEL), row(LANES), pl.BlockSpec((2 * TOP_K, tm), lambda i: (0, i)),
                   const((1, LANES))],
        out_shape=[jax.ShapeDtypeStruct((n, D_MODEL), F32), jax.ShapeDtypeStruct((n, D_MODEL), F32),
                   jax.ShapeDtypeStruct((n, LANES), F32), jax.ShapeDtypeStruct((2 * TOP_K, n), jnp.int32),
                   jax.ShapeDtypeStruct((1, LANES), F32)],
        scratch_shapes=[pltpu.VMEM((1, LANES), F32)],
        compiler_params=_cparams(("arbitrary",)),
        name="post_router",
    )(fo, go, ma, mb, x, wfo, wgo, wout, gate_bias, nw, rw, rb, tri, cnt_in)


def _dispatch_kernel(pad_ref, dest_ref, hp_ref, hs_ref, xs_ref, sem, zbuf, zsem, *, tm, np_tiles):
    i = pl.program_id(0)
    zrows = zbuf.shape[0]

    @pl.when(i == 0)
    def _():
        zbuf[...] = jnp.zeros(zbuf.shape, zbuf.dtype)
        for phase in ("start", "wait"):
            def run(cp, phase=phase):
                cp.start() if phase == "start" else cp.wait()

            def per_expert(e, carry, run=run):
                first = pad_ref[e]
                pos = pad_ref[2 * N_EXPERTS + e]
                groups = pad_ref[3 * N_EXPERTS + e]

                def one(j, c2):
                    run(pltpu.make_async_copy(zbuf.at[pl.ds(0, 1), :], xs_ref.at[pl.ds(first + j, 1), :], zsem))
                    return c2

                lax.fori_loop(0, pad_ref[N_EXPERTS + e], one, 0)
                for b in reversed(range(PAD_GROUP_BITS)):
                    size = ROW_GROUP << b
                    bit = (groups >> b) & 1

                    @pl.when(bit == 1)
                    def _(pos=pos, size=size):
                        run(pltpu.make_async_copy(zbuf.at[pl.ds(0, size), :],
                                                  xs_ref.at[pl.ds(pl.multiple_of(pos, ROW_GROUP), size), :], zsem))

                    pos = pos + bit * size
                return carry

            lax.fori_loop(0, N_EXPERTS, per_expert, 0)

            def tail(j, carry, run=run):
                row = pl.multiple_of(pad_ref[4 * N_EXPERTS] + j * zrows, zrows)
                run(pltpu.make_async_copy(zbuf, xs_ref.at[pl.ds(row, zrows), :], zsem))
                return carry

            lax.fori_loop(0, pad_ref[4 * N_EXPERTS + 1], tail, 0)

    group = min(DMA_ROWS_PER_TRIP, tm)

    def scatter(h_ref):
        def issue(tg, carry):
            t0 = pl.multiple_of(tg * group, group)
            for r in range(group):
                for kk in range(TOP_K):
                    d = dest_ref[(t0 + r) * TOP_K + kk]
                    pltpu.make_async_copy(h_ref.at[pl.ds(t0 + r, 1), :], xs_ref.at[pl.ds(d, 1), :],
                                          sem).start(priority=kk % 2)
            return carry

        lax.fori_loop(0, tm // group, issue, 0)
        for kk in range(TOP_K):
            pltpu.make_async_copy(h_ref, xs_ref.at[pl.ds(0, tm), :], sem).wait()

    @pl.when(i < np_tiles)
    def _():
        scatter(hp_ref)

    @pl.when(i >= np_tiles)
    def _():
        scatter(hs_ref)


def _dispatch(dest_flat, h_p, h_s, pad_tab, rows, tm):
    np_tiles, ns_tiles = h_p.shape[0] // tm, h_s.shape[0] // tm
    return pl.pallas_call(
        functools.partial(_dispatch_kernel, tm=tm, np_tiles=np_tiles),
        grid_spec=pltpu.PrefetchScalarGridSpec(
            num_scalar_prefetch=1, grid=(np_tiles + ns_tiles,),
            in_specs=[pl.BlockSpec((tm * TOP_K,), lambda i, pad: (i,), memory_space=pltpu.SMEM),
                      pl.BlockSpec((tm, D_MODEL), lambda i, pad: (jnp.minimum(i, np_tiles - 1), 0)),
                      pl.BlockSpec((tm, D_MODEL), lambda i, pad: (jnp.maximum(i - np_tiles, 0), 0))],
            out_specs=pl.BlockSpec(memory_space=pl.ANY),
            scratch_shapes=[pltpu.SemaphoreType.DMA(()), pltpu.VMEM((ZERO_ROWS, D_MODEL), F32),
                            pltpu.SemaphoreType.DMA(())]),
        out_shape=jax.ShapeDtypeStruct((rows, D_MODEL), F32),
        compiler_params=_cparams(("arbitrary",)),
        name="moe_dispatch",
    )(pad_tab, dest_flat, h_p, h_s)


def _expert_kernel(be_ref, nu_ref, x_ref, wgu_ref, bgu_ref, wd_ref, bd_ref, y_ref, wgu_b, wd_b):
    i = pl.program_id(0)
    changed = jnp.logical_or(i == 0, be_ref[i] != be_ref[jnp.maximum(i - 1, 0)])

    @pl.when(changed)
    def _():
        for f in range(0, 2 * D_FF, FF_CHUNK):
            wgu_b[:, f:f + FF_CHUNK] = wgu_ref[:, f:f + FF_CHUNK].astype(BF16)
        for f in range(0, D_FF, FF_CHUNK):
            wd_b[f:f + FF_CHUNK, :] = wd_ref[f:f + FF_CHUNK, :].astype(BF16)

    @pl.when(i < nu_ref[0])
    def _():
        x = x_ref[...].astype(BF16)
        acc = jnp.zeros(y_ref.shape, F32)
        for f in range(0, D_FF, FF_CHUNK):
            gate = _dot(x, wgu_b[:, f:f + FF_CHUNK]) + bgu_ref[:, f:f + FF_CHUNK]
            up = _dot(x, wgu_b[:, D_FF + f:D_FF + f + FF_CHUNK]) + bgu_ref[:, D_FF + f:D_FF + f + FF_CHUNK]
            gate = jnp.minimum(gate, SWIGLU_LIMIT)
            up = jnp.clip(up, -SWIGLU_LIMIT, SWIGLU_LIMIT)
            act = (up + 1.0) * (gate * _sigmoid(SWIGLU_ALPHA * gate))
            acc = acc + _dot(act.astype(BF16), wd_b[f:f + FF_CHUNK, :])
        y_ref[...] = acc + bd_ref[...]

    @pl.when(i >= nu_ref[0])
    def _():
        y_ref[...] = jnp.zeros(y_ref.shape, F32)


def _experts(block_e, n_used, xs, w_gu, b_gu, w_down, b_down):
    rows = xs.shape[0]
    tb = EXPERT_BLOCK
    grid_spec = pltpu.PrefetchScalarGridSpec(
        num_scalar_prefetch=2,
        grid=(rows // tb,),
        in_specs=[pl.BlockSpec((tb, D_MODEL), lambda i, be, nu: (jnp.minimum(i, jnp.maximum(nu[0] - 1, 0)), 0)),
                  pl.BlockSpec((None, D_MODEL, 2 * D_FF), lambda i, be, nu: (be[i], 0, 0)),
                  pl.BlockSpec((None, 1, 2 * D_FF), lambda i, be, nu: (be[i], 0, 0)),
                  pl.BlockSpec((None, D_FF, D_MODEL), lambda i, be, nu: (be[i], 0, 0)),
                  pl.BlockSpec((None, 1, D_MODEL), lambda i, be, nu: (be[i], 0, 0))],
        out_specs=pl.BlockSpec((tb, D_MODEL), lambda i, be, nu: (i, 0)),
        scratch_shapes=[pltpu.VMEM((D_MODEL, 2 * D_FF), BF16), pltpu.VMEM((D_FF, D_MODEL), BF16)],
    )
    return pl.pallas_call(
        _expert_kernel,
        grid_spec=grid_spec,
        out_shape=jax.ShapeDtypeStruct((rows, D_MODEL), F32),
        compiler_params=_cparams(("arbitrary",)),
        name="moe_experts",
    )(block_e, n_used, xs, w_gu, b_gu.reshape(N_EXPERTS, 1, 2 * D_FF), w_down, b_down.reshape(N_EXPERTS, 1, D_MODEL))


def _combine_kernel(dest_ref, x2_ref, gates_ref, fw_ref, ys_ref, y_ref, buf, sem, *, tm):
    group = min(DMA_ROWS_PER_TRIP, tm)

    def issue(tg, carry):
        t0 = pl.multiple_of(tg * group, group)
        for r in range(group):
            for kk in range(TOP_K):
                d = dest_ref[(t0 + r) * TOP_K + kk]
                pltpu.make_async_copy(ys_ref.at[pl.ds(d, 1), :], buf.at[kk, pl.ds(t0 + r, 1), :], sem).start(priority=kk % 2)
        return carry

    lax.fori_loop(0, tm // group, issue, 0)
    for kk in range(TOP_K):
        pltpu.make_async_copy(ys_ref.at[pl.ds(0, tm), :], buf.at[kk], sem).wait()
    gates = gates_ref[...]
    out = x2_ref[...]
    for kk in range(TOP_K):
        out = out + gates[:, kk:kk + 1] * buf[kk]
    y_ref[...] = (out * lax.rsqrt(jnp.mean(out * out, axis=-1, keepdims=True) + RMS_EPS)) * fw_ref[...]


def _combine(dest_flat, x2, gates, final_w, ys, tm):
    n = x2.shape[0]
    return pl.pallas_call(
        functools.partial(_combine_kernel, tm=tm),
        grid=(n // tm,),
        in_specs=[pl.BlockSpec((tm * TOP_K,), lambda i: (i,), memory_space=pltpu.SMEM),
                  pl.BlockSpec((tm, D_MODEL), lambda i: (i, 0)),
                  pl.BlockSpec((tm, LANES), lambda i: (i, 0)),
                  pl.BlockSpec((1, D_MODEL), lambda i: (0, 0)),
                  pl.BlockSpec(memory_space=pl.ANY)],
        out_specs=pl.BlockSpec((tm, D_MODEL), lambda i: (i, 0)),
        out_shape=jax.ShapeDtypeStruct((n, D_MODEL), F32),
        scratch_shapes=[pltpu.VMEM((TOP_K, tm, D_MODEL), F32), pltpu.SemaphoreType.DMA(())],
        compiler_params=_cparams(("arbitrary",)),
        name="moe_combine",
    )(dest_flat, x2, gates, final_w, ys)


def _row_tile(n, want):
    t = min(want, n)
    while n % t:
        t //= 2
    return t


def kernel(x_prompt, x_sample, cache_fox_k, cache_fox_v, cache_fox_logf, state_gdn, state_gdn_conv, attn_norm_w, w_in, fox_f_bias, gdn_conv_w, gdn_a_log, gdn_dt_bias, gdn_norm_w, gate_bias, fox_w_o, gdn_w_o, w_out, ffn_norm_w, router_w, router_b, expert_w_gu, expert_b_gu, expert_w_down, expert_b_down, final_norm_w):
    l = 0
    bp, sp, d = x_prompt.shape
    bs, ts, _ = x_sample.shape
    past = cache_fox_k.shape[2]
    n_p, n_s = bp * sp, bs * ts

    w = w_in[l]
    o_ff = 3 * FOX_W
    o_gqkv = o_ff + FOX_HEADS
    o_gz = o_gqkv + GDN_CONV_CH
    o_ga = o_gz + GDN_VW
    o_gb = o_ga + GDN_HEADS
    o_ma = o_gb + GDN_HEADS
    w_r = jnp.concatenate(
        [w[:, :o_ff], w[:, o_gqkv:o_gz], w[:, o_gz:o_ga], w[:, o_ma:], w[:, o_ff:o_gqkv], w[:, o_ga:o_ma],
         jnp.zeros((d, SMALL_W - FOX_HEADS - 2 * GDN_HEADS), w.dtype)], axis=1).astype(BF16)
    sb = jnp.zeros((8, SMALL_W), F32)
    sb = sb.at[0, LOGF_LO:LOGF_LO + FOX_HEADS].set(fox_f_bias[l])
    sb = sb.at[0, G_LO:G_LO + GDN_HEADS].set(gdn_dt_bias[l])
    sb = sb.at[1, G_LO:G_LO + GDN_HEADS].set(gdn_a_log[l])
    anw = attn_norm_w[l].reshape(1, d)
    wfo = fox_w_o[l].astype(BF16)
    wgo = gdn_w_o[l].astype(BF16)
    wout = w_out[l].astype(BF16)
    fnw = ffn_norm_w[l].reshape(1, d)
    rw = jnp.concatenate([router_w[l], jnp.zeros((d, LANES - N_EXPERTS), F32)], axis=1)
    rw_hi = rw.astype(BF16)
    rw = jnp.stack([rw_hi, (rw - rw_hi.astype(F32)).astype(BF16)])
    rb = jnp.concatenate([router_b[l], jnp.full((LANES - N_EXPERTS,), -jnp.inf, F32)]).reshape(1, LANES)
    gnw = gdn_norm_w[l].reshape(1, GDN_DV)
    final_w = final_norm_w.reshape(1, d)

    def mixer(x2d, b, t, aug):
        tm = _row_tile(t if aug else x2d.shape[0], 256)
        q, k, v, kb, vb, gqkv, gz, ma, mb, small = _proj(x2d, anw, w_r, sb, tm, t if aug else None)
        return dict(q=q.reshape(b, t, -1), k=k, v=v, kb=kb.reshape(b, t, -1), vb=vb.reshape(b, t, -1),
                    gqkv=gqkv.reshape(b, t, GDN_CONV_CH), gz=gz.reshape(b, t, GDN_VW), ma=ma, mb=mb,
                    small=small.reshape(b, t, SMALL_W))

    pp = mixer(x_prompt.reshape(n_p, d), bp, sp, True)
    logf_p = pp["small"][:, :, LOGF_LO:LOGF_LO + FOX_HEADS]
    fo_p = _attn_prompt(pp["q"], pp["kb"], pp["vb"], _row_tile(sp, ATTN_TQ), _row_tile(sp, ATTN_TK))
    rows_p = _row_tile(sp, 4 * CHUNK)
    go_p, state_p = _gdn(pp["gqkv"], pp["gz"], pp["small"], jnp.zeros((bp, CONV_W - 1, GDN_CONV_CH), F32),
                         jnp.zeros((bp, GDN_HEADS, GDN_DK, GDN_DV), F32), gdn_conv_w[l], gnw, rows_p)
    conv_p = pp["gqkv"][:, sp - (CONV_W - 1):, :]

    ps = mixer(x_sample.reshape(n_s, d), bs, ts, False)
    logf_s = ps["small"][:, :, LOGF_LO:LOGF_LO + FOX_HEADS]
    tot = past + ts
    tot_pad = -(-tot // LANES) * LANES
    logf_all = jnp.concatenate([cache_fox_logf[l].astype(F32), logf_s,
                                jnp.zeros((bs, tot_pad - tot, FOX_HEADS), F32)], axis=1)
    c_s = _seq_cumsum(jnp.transpose(logf_all, (0, 2, 1)))
    fo_s = _attn_step(ps["q"], ps["kb"], ps["vb"], cache_fox_k[l].reshape(bs, past, FOX_W),
                      cache_fox_v[l].reshape(bs, past, FOX_W), c_s)
    t_pad = -(-ts // CHUNK) * CHUNK
    padt = lambda a: jnp.pad(a, ((0, 0), (0, t_pad - ts), (0, 0)))
    go_s, state_s = _gdn(padt(ps["gqkv"]), padt(ps["gz"]), padt(ps["small"]), state_gdn_conv[l], state_gdn[l],
                         gdn_conv_w[l], gnw, CHUNK)
    go_s = go_s[:, :ts, :]
    conv_s = ps["gqkv"][:, ts - (CONV_W - 1):, :]

    tm_p = _row_tile(n_p, 512)
    tm_s = _row_tile(n_s, 512)
    x2_p, h2_p, gates_p, ir_p, cnt_p = _post(fo_p.reshape(n_p, FOX_W), go_p.reshape(n_p, GDN_VW), pp["ma"], pp["mb"],
                                             x_prompt.reshape(n_p, d), wfo, wgo, wout, gate_bias[l], fnw, rw, rb,
                                             jnp.zeros((1, LANES), F32), tm_p)
    x2_s, h2_s, gates_s, ir_s, cnt_s = _post(fo_s.reshape(n_s, FOX_W), go_s.reshape(n_s, GDN_VW), ps["ma"], ps["mb"],
                                             x_sample.reshape(n_s, d), wfo, wgo, wout, gate_bias[l], fnw, rw, rb,
                                             cnt_p, tm_s)

    tb = EXPERT_BLOCK
    nk = (n_p + n_s) * TOP_K
    n_blocks = -(-nk // tb) + N_EXPERTS
    counts = cnt_s[0, :N_EXPERTS].astype(jnp.int32)
    padded = (counts + tb - 1) // tb * tb
    pad_end = jnp.cumsum(padded)
    pad_start = pad_end - padded
    block_pos = jnp.arange(n_blocks, dtype=jnp.int32) * tb
    block_e = jnp.minimum(jnp.sum((pad_end[None, :] <= block_pos[:, None]).astype(jnp.int32), axis=1), N_EXPERTS - 1)
    n_used = (pad_end[-1:] // tb).astype(jnp.int32)
    dest = lambda ir: jnp.transpose(pad_start[ir[:TOP_K]] + ir[TOP_K:]).reshape(-1)
    dest_p, dest_s = dest(ir_p), dest(ir_s)

    pad_first = pad_start + counts
    pad_aligned = (pad_first + ROW_GROUP - 1) // ROW_GROUP * ROW_GROUP
    rows = n_blocks * tb
    pad_tab = jnp.concatenate([pad_first, pad_aligned - pad_first, pad_aligned, (pad_end - pad_aligned) // ROW_GROUP,
                               pad_end[-1:], (rows - pad_end[-1:]) // ZERO_ROWS]).astype(jnp.int32)
    tm_d = _row_tile(n_s, 512)
    assert n_p % tm_d == 0 and tb % ZERO_ROWS == 0
    xs = _dispatch(jnp.concatenate([dest_p, dest_s]), h2_p, h2_s, pad_tab, rows, tm_d)
    ys = _experts(block_e, n_used, xs, expert_w_gu[l], expert_b_gu[l], expert_w_down[l], expert_b_down[l])
    y_p = _combine(dest_p, x2_p, gates_p, final_w, ys, tm_p)
    y_s = _combine(dest_s, x2_s, gates_s, final_w, ys, tm_s)

    hd = (FOX_HEADS, FOX_HEAD_DIM)
    return (y_p.reshape(bp, sp, d), y_s.reshape(bs, ts, d),
            pp["k"].reshape(1, bp, sp, *hd), pp["v"].reshape(1, bp, sp, *hd), logf_p[None],
            state_p[None], conv_p[None],
            ps["k"].reshape(1, bs, ts, *hd), ps["v"].reshape(1, bs, ts, *hd), logf_s[None],
            state_s[None], conv_s[None])
```

```python
import functools

import numpy as np
import jax
import jax.numpy as jnp
from jax import lax
from jax.experimental import pallas as pl
from jax.experimental.pallas import tpu as pltpu

F32 = jnp.float32
BF16 = jnp.bfloat16
HIGHEST = lax.Precision.HIGHEST

D_MODEL = 1024
FOX_HEADS = 8
FOX_HEAD_DIM = 64
FOX_W = FOX_HEADS * FOX_HEAD_DIM
FOX_SCALE = FOX_HEAD_DIM ** -0.5
GDN_HEADS = 4
GDN_DK = 128
GDN_DV = 128
GDN_KW = GDN_HEADS * GDN_DK
GDN_VW = GDN_HEADS * GDN_DV
GDN_CONV_CH = 2 * GDN_KW + GDN_VW
CONV_W = 4
CHUNK = 64
N_EXPERTS = 32
TOP_K = 4
D_FF = D_MODEL
SWIGLU_LIMIT = 7.0
SWIGLU_ALPHA = 1.702
RMS_EPS = 1e-6
NEG_INF = -1e30

LANES = 128
SMALL_W = LANES
LOGF_LO, G_LO, BETA_LO = 0, FOX_HEADS, FOX_HEADS + GDN_HEADS
C_Q, C_K, C_V = 0, FOX_W, 2 * FOX_W
C_GQKV = 3 * FOX_W
C_GZ = C_GQKV + GDN_CONV_CH
C_MA = C_GZ + GDN_VW
C_MB = C_MA + D_MODEL
C_SMALL = C_MB + D_MODEL
W_R = C_SMALL + SMALL_W

VMEM_LIMIT = 56 * 1024 * 1024
ATTN_TQ = 512
ATTN_TK = 512
ATTN_HEADS_PER_LOOP = 4
ROW_GROUP = 8
DMA_ROWS_PER_TRIP = 8
EXPERT_BLOCK = 512
PAD_GROUP_BITS = (EXPERT_BLOCK // ROW_GROUP).bit_length() - 1
ZERO_ROWS = ROW_GROUP << (PAD_GROUP_BITS - 1)
FF_CHUNK = 512


def _cparams(sem, vmem=VMEM_LIMIT):
    return pltpu.CompilerParams(dimension_semantics=sem, vmem_limit_bytes=vmem)


def _softplus(x):
    return jnp.maximum(x, 0.0) + jnp.log1p(jnp.exp(-jnp.abs(x)))


def _sigmoid(x):
    return jax.nn.sigmoid(x)


def _dot(a, b, precision=None):
    return jnp.dot(a, b, preferred_element_type=F32, precision=precision)


def _dot_nt(a, b, precision=None):
    return lax.dot_general(a, b, (((1,), (1,)), ((), ())), preferred_element_type=F32, precision=precision)


def _dot_tn(a, b, precision=None):
    return lax.dot_general(a, b, (((0,), (0,)), ((), ())), preferred_element_type=F32, precision=precision)


def _head_tiles(x):
    tiles = []
    for pair in range(FOX_HEADS // 2):
        t = x[:, pair * LANES:(pair + 1) * LANES]
        tiles += [t, pltpu.roll(t, FOX_HEAD_DIM, axis=1)]
    return tiles


def _proj_kernel(x_ref, nw_ref, w_ref, sb_ref, *rest, aug, tiles_per_seq):
    if aug:
        tri_ref, place_ref, q_ref, k_ref, v_ref, kb_ref, vb_ref, gqkv_ref, gz_ref, ma_ref, mb_ref, small_ref, carry = rest
    else:
        q_ref, k_ref, v_ref, kb_ref, vb_ref, gqkv_ref, gz_ref, ma_ref, mb_ref, small_ref = rest
    x = x_ref[...]
    ms = jnp.mean(x * x, axis=-1, keepdims=True)
    h = ((x * lax.rsqrt(ms + RMS_EPS)) * nw_ref[...]).astype(BF16)

    def sec(lo, width):
        return _dot(h, w_ref[:, lo:lo + width])

    q = sec(C_Q, FOX_W) * FOX_SCALE
    k = sec(C_K, FOX_W)
    v = sec(C_V, FOX_W)
    if aug:
        k_ref[...] = jnp.transpose(k)
        v_ref[...] = jnp.transpose(v)
    else:
        k_ref[...] = k
        v_ref[...] = v
    gqkv_ref[...] = sec(C_GQKV, GDN_CONV_CH)
    gz_ref[...] = sec(C_GZ, GDN_VW)
    ma_ref[...] = sec(C_MA, D_MODEL)
    mb_ref[...] = sec(C_MB, D_MODEL)
    z = sec(C_SMALL, SMALL_W) + sb_ref[0:1, :]
    lane = lax.broadcasted_iota(jnp.int32, z.shape, 1)
    logf = -_softplus(-z)
    g = -jnp.exp(sb_ref[1:2, :]) * _softplus(z)
    beta = _sigmoid(z)
    small_ref[...] = jnp.where(lane < G_LO, logf, jnp.where(lane < BETA_LO, g, beta))
    if not aug:
        q_ref[...] = q.astype(BF16)
        kb_ref[...] = k.astype(BF16)
        vb_ref[...] = v.astype(BF16)
        return

    @pl.when(pl.program_id(0) % tiles_per_seq == 0)
    def _():
        carry[...] = jnp.zeros(carry.shape, F32)

    lf = jnp.where(lane < G_LO, logf, 0.0)
    l1 = lf.astype(BF16)
    r1 = lf - l1.astype(F32)
    l2 = r1.astype(BF16)
    l3 = (r1 - l2.astype(F32)).astype(BF16)
    tri = tri_ref[...]
    c = (_dot(tri, l1) + (_dot(tri, l2) + _dot(tri, l3))) + carry[...]
    carry[...] = c[c.shape[0] - 1:, :]
    nc = -c
    p1 = nc.astype(BF16).astype(F32)
    r1 = nc - p1
    p2 = r1.astype(BF16).astype(F32)
    p3 = (r1 - p2).astype(BF16).astype(F32)
    parts = p1 + pltpu.roll(p2, FOX_HEADS, axis=1) + pltpu.roll(p3, 2 * FOX_HEADS, axis=1)
    ext = _dot(parts.astype(BF16), place_ref[...])
    low = lane < FOX_HEAD_DIM
    ones3 = jnp.where((lane >= FOX_HEAD_DIM) & (lane < FOX_HEAD_DIM + 3), 1.0, 0.0)
    one1 = jnp.where(lane == FOX_HEAD_DIM, 1.0, 0.0)
    for hd, (qt, kt, vt) in enumerate(zip(_head_tiles(q), _head_tiles(k), _head_tiles(v))):
        cols = slice(hd * LANES, (hd + 1) * LANES)
        q_ref[:, cols] = jnp.where(low, qt, ones3).astype(BF16)
        kb_ref[:, cols] = jnp.where(low, kt, ext[:, cols]).astype(BF16)
        vb_ref[:, cols] = jnp.where(low, vt, one1).astype(BF16)


def _proj(x2d, nw, w_r, sb, tm, seq=None):
    n = x2d.shape[0]
    aug = seq is not None
    row = lambda w: pl.BlockSpec((tm, w), lambda i: (i, 0))
    const = lambda shape: pl.BlockSpec(shape, lambda i: (0, 0))
    aw = FOX_HEADS * LANES if aug else FOX_W
    outs = [(aw, BF16), (FOX_W, F32), (FOX_W, F32), (aw, BF16), (aw, BF16), (GDN_CONV_CH, F32),
            (GDN_VW, F32), (D_MODEL, F32), (D_MODEL, F32), (SMALL_W, F32)]
    in_specs = [row(D_MODEL), const((1, D_MODEL)), const((D_MODEL, W_R)), const((8, SMALL_W))]
    args = [x2d, nw, w_r, sb]
    scratch = []
    tps = seq // tm if aug else 1
    head_spec = pl.BlockSpec((None, FOX_W, tm), lambda i: (i // tps, 0, i % tps))
    head_shape = jax.ShapeDtypeStruct((n // seq if aug else 1, FOX_W, seq if aug else tm), F32)
    if aug:
        place = np.zeros((LANES, FOX_HEADS * LANES), np.float32)
        for hd in range(FOX_HEADS):
            for j in range(3):
                place[j * FOX_HEADS + hd, hd * LANES + FOX_HEAD_DIM + j] = 1.0
        in_specs += [const((tm, tm)), const((LANES, FOX_HEADS * LANES))]
        args += [jnp.asarray(np.tril(np.ones((tm, tm), np.float32)), BF16), jnp.asarray(place, BF16)]
        scratch = [pltpu.VMEM((1, SMALL_W), F32)]
    return pl.pallas_call(
        functools.partial(_proj_kernel, aug=aug, tiles_per_seq=(seq // tm if aug else 1)),
        grid=(n // tm,),
        in_specs=in_specs,
        out_specs=[head_spec if aug and idx in (1, 2) else row(w) for idx, (w, _) in enumerate(outs)],
        out_shape=[head_shape if aug and idx in (1, 2) else jax.ShapeDtypeStruct((n, w), dt)
                   for idx, (w, dt) in enumerate(outs)],
        scratch_shapes=scratch,
        compiler_params=_cparams(("arbitrary",)),
        name="proj",
    )(*args)


def _cumsum_kernel(x_ref, tri_ref, lmat_ref, c_ref):
    x = x_ref[...]
    cb = _dot(x, tri_ref[...], HIGHEST)
    tot = jnp.broadcast_to(cb[:, LANES - 1:LANES], cb.shape)
    c_ref[...] = cb + _dot(lmat_ref[...], tot, HIGHEST)


def _seq_cumsum(logf_bhs):
    b, h, s = logf_bhs.shape
    nb = s // LANES
    r = h * nb
    tri = jnp.asarray(np.triu(np.ones((LANES, LANES), np.float32)))
    rr = np.arange(r)
    lmat = jnp.asarray(((rr[:, None] // nb == rr[None, :] // nb) & (rr[None, :] < rr[:, None])).astype(np.float32))
    out = pl.pallas_call(
        _cumsum_kernel,
        grid=(b,),
        in_specs=[pl.BlockSpec((None, r, LANES), lambda i: (i, 0, 0)),
                  pl.BlockSpec((LANES, LANES), lambda i: (0, 0)),
                  pl.BlockSpec((r, r), lambda i: (0, 0))],
        out_specs=pl.BlockSpec((None, r, LANES), lambda i: (i, 0, 0)),
        out_shape=jax.ShapeDtypeStruct((b, r, LANES), F32),
        compiler_params=_cparams(("parallel",)),
        name="logf_cumsum",
    )(logf_bhs.reshape(b, r, LANES), tri, lmat)
    return out.reshape(b, h, s)


def _attn_prompt_kernel(q_ref, k_ref, v_ref, o_ref, *, tq, tk):
    i = pl.program_id(1)
    q0 = pl.multiple_of(i * tq, tq)
    n_full = (i * tq) // tk
    lane = lax.broadcasted_iota(jnp.int32, (1, LANES), 1)
    rowi = lax.broadcasted_iota(jnp.int32, (tq, tk), 0)
    coli = lax.broadcasted_iota(jnp.int32, (tq, tk), 1)
    outs = []
    for g0 in range(0, FOX_HEADS, ATTN_HEADS_PER_LOOP):
        heads = list(range(g0, g0 + ATTN_HEADS_PER_LOOP))

        def step(j, carry, mask, heads=heads):
            k0 = pl.multiple_of(j * tk, tk)
            keep = (rowi + q0 >= coli + k0) if mask else None
            ss = [_dot_nt(q_ref[:, h * LANES:(h + 1) * LANES], k_ref[pl.ds(k0, tk), h * LANES:(h + 1) * LANES])
                  for h in heads]
            out = []
            for (m, acc), s, h in zip(carry, ss, heads):
                if mask:
                    s = jnp.where(keep, s, NEG_INF)
                m_new = jnp.maximum(m, jnp.max(s, axis=-1, keepdims=True))
                p = jnp.exp(s - m_new).astype(BF16)
                acc = jnp.exp(m - m_new) * acc + _dot(p, v_ref[pl.ds(k0, tk), h * LANES:(h + 1) * LANES])
                out.append((m_new, acc))
            return tuple(out)

        init = tuple((jnp.full((tq, 1), NEG_INF, F32), jnp.zeros((tq, LANES), F32)) for _ in heads)
        carry = lax.fori_loop(0, n_full, lambda j, cr, step=step: step(j, cr, False), init)
        for jj in range(max(tq // tk, 1)):
            carry = step(n_full + jj, carry, True)
        for _, acc in carry:
            outs.append(acc * (1.0 / acc[:, FOX_HEAD_DIM:FOX_HEAD_DIM + 1]))
    for pair in range(FOX_HEADS // 2):
        o_ref[:, pair * LANES:(pair + 1) * LANES] = jnp.where(
            lane < FOX_HEAD_DIM, outs[2 * pair], pltpu.roll(outs[2 * pair + 1], FOX_HEAD_DIM, axis=1)).astype(BF16)


def _attn_prompt(qa, ka, va, tq, tk):
    b, s, aw = qa.shape
    return pl.pallas_call(
        functools.partial(_attn_prompt_kernel, tq=tq, tk=tk),
        grid=(b, s // tq),
        in_specs=[pl.BlockSpec((None, tq, aw), lambda bi, i: (bi, i, 0)),
                  pl.BlockSpec((None, s, aw), lambda bi, i: (bi, 0, 0)),
                  pl.BlockSpec((None, s, aw), lambda bi, i: (bi, 0, 0))],
        out_specs=pl.BlockSpec((None, tq, FOX_W), lambda bi, i: (bi, i, 0)),
        out_shape=jax.ShapeDtypeStruct((b, s, FOX_W), BF16),
        compiler_params=_cparams(("parallel", "arbitrary")),
        name="attn_prompt",
    )(qa, ka, va)


def _attn_step_kernel(q_ref, kn_ref, vn_ref, kct_ref, vct_ref, c_ref, o_ref, *, past, t):
    rowi = lax.broadcasted_iota(jnp.int32, (t, t), 0)
    coli = lax.broadcasted_iota(jnp.int32, (t, t), 1)
    causal = rowi >= coli
    for h in range(FOX_HEADS):
        cols = slice(h * FOX_HEAD_DIM, (h + 1) * FOX_HEAD_DIM)
        qh = q_ref[:, cols]
        c_c = c_ref[h:h + 1, 0:past]
        c_n = c_ref[h:h + 1, past:past + t]
        c0 = c_n[:, 0:1]
        s_c = _dot(qh, kct_ref[h].astype(BF16)) + (c0 - c_c)
        s_n = jnp.where(causal, _dot_nt(qh, kn_ref[:, cols]) + (c0 - c_n), NEG_INF)
        m = jnp.maximum(jnp.max(s_c, axis=-1, keepdims=True), jnp.max(s_n, axis=-1, keepdims=True))
        p_c = jnp.exp(s_c - m)
        p_n = jnp.exp(s_n - m)
        l = jnp.sum(p_c, axis=-1, keepdims=True) + jnp.sum(p_n, axis=-1, keepdims=True)
        acc = _dot_nt(p_c.astype(BF16), vct_ref[h].astype(BF16)) + _dot(p_n.astype(BF16), vn_ref[:, cols])
        o_ref[:, cols] = (acc * (1.0 / l)).astype(BF16)


def _attn_step(qb, knb, vnb, kc_t, vc_t, c_all):
    b, t, _ = qb.shape
    past = kc_t.shape[-1]
    sp = c_all.shape[-1]
    new = pl.BlockSpec((None, t, FOX_W), lambda bi: (bi, 0, 0))
    old = pl.BlockSpec((None, FOX_HEADS, FOX_HEAD_DIM, past), lambda bi: (bi, 0, 0, 0))
    return pl.pallas_call(
        functools.partial(_attn_step_kernel, past=past, t=t),
        grid=(b,),
        in_specs=[new, new, new, old, old, pl.BlockSpec((None, FOX_HEADS, sp), lambda bi: (bi, 0, 0))],
        out_specs=new,
        out_shape=jax.ShapeDtypeStruct((b, t, FOX_W), BF16),
        compiler_params=_cparams(("parallel",)),
        name="attn_step",
    )(qb, knb, vnb, kc_t, vc_t, c_all)


def _split(a):
    hi = a.astype(BF16)
    return hi, (a - hi.astype(F32)).astype(BF16)


def _dot3(a_parts, b_parts):
    (ah, al), (bh, bl) = a_parts, b_parts
    return _dot(ah, bh) + (_dot(ah, bl) + _dot(al, bh))


def _gdn_kernel(x_ref, z_ref, sm_ref, cb_ref, s0_ref, cw_ref, nw_ref, o_ref, s_ref, xbuf, *, rows):
    c = pl.program_id(1)
    L = CHUNK

    @pl.when(c == 0)
    def _():
        xbuf[8 - (CONV_W - 1):8, :] = cb_ref[...]
        s_ref[...] = s0_ref[...]

    xbuf[8:8 + rows, :] = x_ref[...]
    conv = xbuf[5:5 + rows, :] * cw_ref[0:1, :]
    for i in range(1, CONV_W):
        conv = conv + xbuf[5 + i:5 + i + rows, :] * cw_ref[i:i + 1, :]
    xbuf[5:8, :] = xbuf[rows + 5:rows + 8, :]
    conv = conv * _sigmoid(conv)

    ri = lax.broadcasted_iota(jnp.int32, (L, L), 0)
    ci = lax.broadcasted_iota(jnp.int32, (L, L), 1)
    tri_incl = ri >= ci
    tri_strict = ri > ci
    eye = ri == ci
    eye_f = eye.astype(F32)
    tril_b = tri_incl.astype(BF16)

    units = []
    for ch in range(rows // L):
        r0 = ch * L
        sm = sm_ref[r0:r0 + L, :]
        sh, sl = _split(sm)
        sl2 = (sm - sh.astype(F32) - sl.astype(F32)).astype(BF16)
        gcum = _dot(tril_b, sh) + (_dot(tril_b, sl) + _dot(tril_b, sl2))
        for h in range(GDN_HEADS):
            xq = conv[r0:r0 + L, h * GDN_DK:(h + 1) * GDN_DK]
            xk = conv[r0:r0 + L, GDN_KW + h * GDN_DK:GDN_KW + (h + 1) * GDN_DK]
            v = conv[r0:r0 + L, 2 * GDN_KW + h * GDN_DV:2 * GDN_KW + (h + 1) * GDN_DV]
            q = xq * lax.rsqrt(jnp.sum(xq * xq, axis=-1, keepdims=True) + RMS_EPS) * (GDN_DK ** -0.5)
            k = xk * lax.rsqrt(jnp.sum(xk * xk, axis=-1, keepdims=True) + RMS_EPS)
            beta = sm[:, BETA_LO + h:BETA_LO + h + 1]
            gc = gcum[:, G_LO + h:G_LO + h + 1]
            grow = jnp.sum(jnp.where(eye, gc, 0.0), axis=0, keepdims=True)
            decay = jnp.where(tri_incl, jnp.exp(jnp.where(tri_incl, gc - grow, 0.0)), 0.0)
            eg = jnp.exp(gc)
            glast = gc[L - 1:L, :]
            kb = k.astype(BF16)
            qb = q.astype(BF16)
            a = jnp.where(tri_strict, _dot_nt(kb, kb) * decay, 0.0) * beta
            units.append(dict(r0=r0, h=h, a=a, eg=eg, eglast=jnp.exp(glast), qb=qb,
                              rhs=jnp.concatenate([beta * v, (beta * eg) * k], axis=1),
                              qk=(_dot_nt(qb, kb) * decay).astype(BF16),
                              kd=(k * jnp.exp(glast - gc)).astype(BF16)))
    xs = [eye_f - u["a"] for u in units]
    pp = [_split(u["a"]) for u in units]
    m = 2
    while m < L:
        pp = [_split(_dot3(p2, p2)) for p2 in pp]
        xs = [x + _dot3(_split(x), p2) for x, p2 in zip(xs, pp)]
        m *= 2
    for u, x in zip(units, xs):
        uw = _dot3(_split(x), _split(u["rhs"]))
        u["u"] = uw[:, :GDN_DV]
        u["qw"] = jnp.concatenate([u["qb"], uw[:, GDN_DV:].astype(BF16)], axis=0)

    state = [s_ref[h] for h in range(GDN_HEADS)]
    for u in units:
        h, r0 = u["h"], u["r0"]
        s = state[h]
        r = _dot(u["qw"], s.astype(BF16))
        db = (u["u"] - r[L:, :]).astype(BF16)
        o = u["eg"] * r[:L, :] + _dot(u["qk"], db)
        state[h] = u["eglast"] * s + _dot_tn(u["kd"], db)
        on = o * lax.rsqrt(jnp.mean(o * o, axis=-1, keepdims=True) + RMS_EPS) * nw_ref[...]
        zz = z_ref[r0:r0 + L, h * GDN_DV:(h + 1) * GDN_DV]
        o_ref[r0:r0 + L, h * GDN_DV:(h + 1) * GDN_DV] = (on * (zz * _sigmoid(zz))).astype(BF16)
    for h in range(GDN_HEADS):
        s_ref[h] = state[h]


def _gdn(gqkv, gz, small, conv_buf, s0, conv_w, norm_w, rows):
    b, t, _ = gqkv.shape
    blk = lambda w: pl.BlockSpec((None, rows, w), lambda bi, c: (bi, c, 0))
    state = pl.BlockSpec((None, GDN_HEADS, GDN_DK, GDN_DV), lambda bi, c: (bi, 0, 0, 0))
    return pl.pallas_call(
        functools.partial(_gdn_kernel, rows=rows),
        grid=(b, t // rows),
        in_specs=[blk(GDN_CONV_CH), blk(GDN_VW), blk(SMALL_W),
                  pl.BlockSpec((None, CONV_W - 1, GDN_CONV_CH), lambda bi, c: (bi, 0, 0)),
                  state,
                  pl.BlockSpec((CONV_W, GDN_CONV_CH), lambda bi, c: (0, 0)),
                  pl.BlockSpec((1, GDN_DV), lambda bi, c: (0, 0))],
        out_specs=[blk(GDN_VW), state],
        out_shape=[jax.ShapeDtypeStruct((b, t, GDN_VW), BF16),
                   jax.ShapeDtypeStruct((b, GDN_HEADS, GDN_DK, GDN_DV), F32)],
        scratch_shapes=[pltpu.VMEM((rows + 8, GDN_CONV_CH), F32)],
        compiler_params=_cparams(("parallel", "arbitrary")),
        name="gdn",
    )(gqkv, gz, small, conv_buf, s0, conv_w, norm_w)


def _post_kernel(fo_ref, go_ref, ma_ref, mb_ref, x_ref, wfo_ref, wgo_ref, wout_ref, gb_ref, nw_ref, rw_ref, rb_ref,
                 tri_ref, cin_ref, x2_ref, h2_ref, gates_ref, ir_ref, cnt_ref, carry):
    i = pl.program_id(0)

    @pl.when(i == 0)
    def _():
        carry[...] = cin_ref[...]

    ya = _dot(fo_ref[...], wfo_ref[...])
    yb = _dot(go_ref[...], wgo_ref[...])
    merged = _sigmoid(ma_ref[...] + gb_ref[0:1, :]) * ya + _sigmoid(mb_ref[...] + gb_ref[1:2, :]) * yb
    x2 = x_ref[...] + _dot(merged.astype(BF16), wout_ref[...])
    x2_ref[...] = x2
    h2 = (x2 * lax.rsqrt(jnp.mean(x2 * x2, axis=-1, keepdims=True) + RMS_EPS)) * nw_ref[...]
    h2_ref[...] = h2
    logits = _dot3(_split(h2), (rw_ref[0], rw_ref[1])) + rb_ref[...]
    lane = lax.broadcasted_iota(jnp.int32, logits.shape, 1).astype(F32)
    work = logits
    vals, hits = [], []
    for _ in range(TOP_K):
        m = jnp.max(work, axis=-1, keepdims=True)
        idx = jnp.min(jnp.where(work == m, lane, float(LANES)), axis=-1, keepdims=True)
        hit = lane == idx
        vals.append(m)
        hits.append((hit, idx))
        work = jnp.where(hit, -jnp.inf, work)
    es = [jnp.exp(v - vals[0]) for v in vals]
    denom = es[0] + es[1] + es[2] + es[3]
    cnt = jnp.zeros(logits.shape, F32)
    for hit, _ in hits:
        cnt = cnt + hit.astype(F32)
    base = _dot(tri_ref[...], cnt.astype(BF16)) + carry[...]
    gates = jnp.zeros(logits.shape, F32)
    ir = jnp.zeros(logits.shape, F32)
    for kk, (hit, idx) in enumerate(hits):
        rank = jnp.sum(jnp.where(hit, base, 0.0), axis=-1, keepdims=True)
        gates = gates + jnp.where(lane == float(kk), es[kk] / denom, 0.0)
        ir = ir + jnp.where(lane == float(kk), idx, 0.0) + jnp.where(lane == float(TOP_K + kk), rank, 0.0)
    gates_ref[...] = gates
    ir_ref[...] = jnp.transpose(ir)[0:2 * TOP_K, :].astype(jnp.int32)
    carry[...] = carry[...] + jnp.sum(cnt, axis=0, keepdims=True)
    cnt_ref[...] = carry[...]


def _post(fo, go, ma, mb, x, wfo, wgo, wout, gate_bias, nw, rw, rb, cnt_in, tm):
    n = x.shape[0]
    row = lambda w: pl.BlockSpec((tm, w), lambda i: (i, 0))
    const = lambda shape: pl.BlockSpec(shape, lambda i: (0, 0))
    tri = jnp.asarray(np.tril(np.ones((tm, tm), np.float32), -1), BF16)
    return pl.pallas_call(
        _post_kernel,
        grid=(n // tm,),
        in_specs=[row(FOX_W), row(GDN_VW), row(D_MODEL), row(D_MODEL), row(D_MODEL),
                  const((FOX_W, D_MODEL)), const((GDN_VW, D_MODEL)), const((D_MODEL, D_MODEL)),
                  const((2, D_MODEL)), const((1, D_MODEL)),
                  pl.BlockSpec((2, D_MODEL, LANES), lambda i: (0, 0, 0)), const((1, LANES)),
                  const((tm, tm)), const((1, LANES))],
        out_specs=[row(D_MODEL), row(D_MODEL), row(LANES), pl.BlockSpec((2 * TOP_K, tm), lambda i: (0, i)),
                   const((1, LANES))],
        out_shape=[jax.ShapeDtypeStruct((n, D_MODEL), F32), jax.ShapeDtypeStruct((n, D_MODEL), F32),
                   jax.ShapeDtypeStruct((n, LANES), F32), jax.ShapeDtypeStruct((2 * TOP_K, n), jnp.int32),
                   jax.ShapeDtypeStruct((1, LANES), F32)],
        scratch_shapes=[pltpu.VMEM((1, LANES), F32)],
        compiler_params=_cparams(("arbitrary",)),
        name="post_router",
    )(fo, go, ma, mb, x, wfo, wgo, wout, gate_bias, nw, rw, rb, tri, cnt_in)


def _dispatch_kernel(pad_ref, dest_ref, hp_ref, hs_ref, xs_ref, sem, zbuf, zsem, *, tm, np_tiles):
    i = pl.program_id(0)
    zrows = zbuf.shape[0]

    @pl.when(i == 0)
    def _():
        zbuf[...] = jnp.zeros(zbuf.shape, zbuf.dtype)
        for phase in ("start", "wait"):
            def run(cp, phase=phase):
                cp.start() if phase == "start" else cp.wait()

            def per_expert(e, carry, run=run):
                first = pad_ref[e]
                pos = pad_ref[2 * N_EXPERTS + e]
                groups = pad_ref[3 * N_EXPERTS + e]

                def one(j, c2):
                    run(pltpu.make_async_copy(zbuf.at[pl.ds(0, 1), :], xs_ref.at[pl.ds(first + j, 1), :], zsem))
                    return c2

                lax.fori_loop(0, pad_ref[N_EXPERTS + e], one, 0)
                for b in reversed(range(PAD_GROUP_BITS)):
                    size = ROW_GROUP << b
                    bit = (groups >> b) & 1

                    @pl.when(bit == 1)
                    def _(pos=pos, size=size):
                        run(pltpu.make_async_copy(zbuf.at[pl.ds(0, size), :],
                                                  xs_ref.at[pl.ds(pl.multiple_of(pos, ROW_GROUP), size), :], zsem))

                    pos = pos + bit * size
                return carry

            lax.fori_loop(0, N_EXPERTS, per_expert, 0)

            def tail(j, carry, run=run):
                row = pl.multiple_of(pad_ref[4 * N_EXPERTS] + j * zrows, zrows)
                run(pltpu.make_async_copy(zbuf, xs_ref.at[pl.ds(row, zrows), :], zsem))
                return carry

            lax.fori_loop(0, pad_ref[4 * N_EXPERTS + 1], tail, 0)

    group = min(DMA_ROWS_PER_TRIP, tm)

    def scatter(h_ref):
        def issue(tg, carry):
            t0 = pl.multiple_of(tg * group, group)
            for r in range(group):
                for kk in range(TOP_K):
                    d = dest_ref[(t0 + r) * TOP_K + kk]
                    pltpu.make_async_copy(h_ref.at[pl.ds(t0 + r, 1), :], xs_ref.at[pl.ds(d, 1), :],
                                          sem).start(priority=kk % 2)
            return carry

        lax.fori_loop(0, tm // group, issue, 0)
        for kk in range(TOP_K):
            pltpu.make_async_copy(h_ref, xs_ref.at[pl.ds(0, tm), :], sem).wait()

    @pl.when(i < np_tiles)
    def _():
        scatter(hp_ref)

    @pl.when(i >= np_tiles)
    def _():
        scatter(hs_ref)


def _dispatch(dest_flat, h_p, h_s, pad_tab, rows, tm):
    np_tiles, ns_tiles = h_p.shape[0] // tm, h_s.shape[0] // tm
    return pl.pallas_call(
        functools.partial(_dispatch_kernel, tm=tm, np_tiles=np_tiles),
        grid_spec=pltpu.PrefetchScalarGridSpec(
            num_scalar_prefetch=1, grid=(np_tiles + ns_tiles,),
            in_specs=[pl.BlockSpec((tm * TOP_K,), lambda i, pad: (i,), memory_space=pltpu.SMEM),
                      pl.BlockSpec((tm, D_MODEL), lambda i, pad: (jnp.minimum(i, np_tiles - 1), 0)),
                      pl.BlockSpec((tm, D_MODEL), lambda i, pad: (jnp.maximum(i - np_tiles, 0), 0))],
            out_specs=pl.BlockSpec(memory_space=pl.ANY),
            scratch_shapes=[pltpu.SemaphoreType.DMA(()), pltpu.VMEM((ZERO_ROWS, D_MODEL), F32),
                            pltpu.SemaphoreType.DMA(())]),
        out_shape=jax.ShapeDtypeStruct((rows, D_MODEL), F32),
        compiler_params=_cparams(("arbitrary",)),
        name="moe_dispatch",
    )(pad_tab, dest_flat, h_p, h_s)


def _expert_kernel(be_ref, nu_ref, x_ref, wgu_ref, bgu_ref, wd_ref, bd_ref, y_ref, wgu_b, wd_b):
    i = pl.program_id(0)
    changed = jnp.logical_or(i == 0, be_ref[i] != be_ref[jnp.maximum(i - 1, 0)])

    @pl.when(changed)
    def _():
        for f in range(0, 2 * D_FF, FF_CHUNK):
            wgu_b[:, f:f + FF_CHUNK] = wgu_ref[:, f:f + FF_CHUNK].astype(BF16)
        for f in range(0, D_FF, FF_CHUNK):
            wd_b[f:f + FF_CHUNK, :] = wd_ref[f:f + FF_CHUNK, :].astype(BF16)

    @pl.when(i < nu_ref[0])
    def _():
        x = x_ref[...].astype(BF16)
        acc = jnp.zeros(y_ref.shape, F32)
        for f in range(0, D_FF, FF_CHUNK):
            gate = _dot(x, wgu_b[:, f:f + FF_CHUNK]) + bgu_ref[:, f:f + FF_CHUNK]
            up = _dot(x, wgu_b[:, D_FF + f:D_FF + f + FF_CHUNK]) + bgu_ref[:, D_FF + f:D_FF + f + FF_CHUNK]
            gate = jnp.minimum(gate, SWIGLU_LIMIT)
            up = jnp.clip(up, -SWIGLU_LIMIT, SWIGLU_LIMIT)
            act = (up + 1.0) * (gate * _sigmoid(SWIGLU_ALPHA * gate))
            acc = acc + _dot(act.astype(BF16), wd_b[f:f + FF_CHUNK, :])
        y_ref[...] = acc + bd_ref[...]

    @pl.when(i >= nu_ref[0])
    def _():
        y_ref[...] = jnp.zeros(y_ref.shape, F32)


def _experts(block_e, n_used, xs, w_gu, b_gu, w_down, b_down):
    rows = xs.shape[0]
    tb = EXPERT_BLOCK
    grid_spec = pltpu.PrefetchScalarGridSpec(
        num_scalar_prefetch=2,
        grid=(rows // tb,),
        in_specs=[pl.BlockSpec((tb, D_MODEL), lambda i, be, nu: (jnp.minimum(i, jnp.maximum(nu[0] - 1, 0)), 0)),
                  pl.BlockSpec((None, D_MODEL, 2 * D_FF), lambda i, be, nu: (be[i], 0, 0)),
                  pl.BlockSpec((None, 1, 2 * D_FF), lambda i, be, nu: (be[i], 0, 0)),
                  pl.BlockSpec((None, D_FF, D_MODEL), lambda i, be, nu: (be[i], 0, 0)),
                  pl.BlockSpec((None, 1, D_MODEL), lambda i, be, nu: (be[i], 0, 0))],
        out_specs=pl.BlockSpec((tb, D_MODEL), lambda i, be, nu: (i, 0)),
        scratch_shapes=[pltpu.VMEM((D_MODEL, 2 * D_FF), BF16), pltpu.VMEM((D_FF, D_MODEL), BF16)],
    )
    return pl.pallas_call(
        _expert_kernel,
        grid_spec=grid_spec,
        out_shape=jax.ShapeDtypeStruct((rows, D_MODEL), F32),
        compiler_params=_cparams(("arbitrary",)),
        name="moe_experts",
    )(block_e, n_used, xs, w_gu, b_gu.reshape(N_EXPERTS, 1, 2 * D_FF), w_down, b_down.reshape(N_EXPERTS, 1, D_MODEL))


def _combine_kernel(dest_ref, x2_ref, gates_ref, fw_ref, ys_ref, y_ref, buf, sem, *, tm):
    group = min(DMA_ROWS_PER_TRIP, tm)

    def issue(tg, carry):
        t0 = pl.multiple_of(tg * group, group)
        for r in range(group):
            for kk in range(TOP_K):
                d = dest_ref[(t0 + r) * TOP_K + kk]
                pltpu.make_async_copy(ys_ref.at[pl.ds(d, 1), :], buf.at[kk, pl.ds(t0 + r, 1), :], sem).start(priority=kk % 2)
        return carry

    lax.fori_loop(0, tm // group, issue, 0)
    for kk in range(TOP_K):
        pltpu.make_async_copy(ys_ref.at[pl.ds(0, tm), :], buf.at[kk], sem).wait()
    gates = gates_ref[...]
    out = x2_ref[...]
    for kk in range(TOP_K):
        out = out + gates[:, kk:kk + 1] * buf[kk]
    y_ref[...] = (out * lax.rsqrt(jnp.mean(out * out, axis=-1, keepdims=True) + RMS_EPS)) * fw_ref[...]


def _combine(dest_flat, x2, gates, final_w, ys, tm):
    n = x2.shape[0]
    return pl.pallas_call(
        functools.partial(_combine_kernel, tm=tm),
        grid=(n // tm,),
        in_specs=[pl.BlockSpec((tm * TOP_K,), lambda i: (i,), memory_space=pltpu.SMEM),
                  pl.BlockSpec((tm, D_MODEL), lambda i: (i, 0)),
                  pl.BlockSpec((tm, LANES), lambda i: (i, 0)),
                  pl.BlockSpec((1, D_MODEL), lambda i: (0, 0)),
                  pl.BlockSpec(memory_space=pl.ANY)],
        out_specs=pl.BlockSpec((tm, D_MODEL), lambda i: (i, 0)),
        out_shape=jax.ShapeDtypeStruct((n, D_MODEL), F32),
        scratch_shapes=[pltpu.VMEM((TOP_K, tm, D_MODEL), F32), pltpu.SemaphoreType.DMA(())],
        compiler_params=_cparams(("arbitrary",)),
        name="moe_combine",
    )(dest_flat, x2, gates, final_w, ys)


def _row_tile(n, want):
    t = min(want, n)
    while n % t:
        t //= 2
    return t


def kernel(x_prompt, x_sample, cache_fox_k, cache_fox_v, cache_fox_logf, state_gdn, state_gdn_conv, attn_norm_w, w_in, fox_f_bias, gdn_conv_w, gdn_a_log, gdn_dt_bias, gdn_norm_w, gate_bias, fox_w_o, gdn_w_o, w_out, ffn_norm_w, router_w, router_b, expert_w_gu, expert_b_gu, expert_w_down, expert_b_down, final_norm_w):
    l = 0
    bp, sp, d = x_prompt.shape
    bs, ts, _ = x_sample.shape
    past = cache_fox_k.shape[2]
    n_p, n_s = bp * sp, bs * ts

    w = w_in[l]
    o_ff = 3 * FOX_W
    o_gqkv = o_ff + FOX_HEADS
    o_gz = o_gqkv + GDN_CONV_CH
    o_ga = o_gz + GDN_VW
    o_gb = o_ga + GDN_HEADS
    o_ma = o_gb + GDN_HEADS
    w_r = jnp.concatenate(
        [w[:, :o_ff], w[:, o_gqkv:o_gz], w[:, o_gz:o_ga], w[:, o_ma:], w[:, o_ff:o_gqkv], w[:, o_ga:o_ma],
         jnp.zeros((d, SMALL_W - FOX_HEADS - 2 * GDN_HEADS), w.dtype)], axis=1).astype(BF16)
    sb = jnp.zeros((8, SMALL_W), F32)
    sb = sb.at[0, LOGF_LO:LOGF_LO + FOX_HEADS].set(fox_f_bias[l])
    sb = sb.at[0, G_LO:G_LO + GDN_HEADS].set(gdn_dt_bias[l])
    sb = sb.at[1, G_LO:G_LO + GDN_HEADS].set(gdn_a_log[l])
    anw = attn_norm_w[l].reshape(1, d)
    wfo = fox_w_o[l].astype(BF16)
    wgo = gdn_w_o[l].astype(BF16)
    wout = w_out[l].astype(BF16)
    fnw = ffn_norm_w[l].reshape(1, d)
    rw = jnp.concatenate([router_w[l], jnp.zeros((d, LANES - N_EXPERTS), F32)], axis=1)
    rw_hi = rw.astype(BF16)
    rw = jnp.stack([rw_hi, (rw - rw_hi.astype(F32)).astype(BF16)])
    rb = jnp.concatenate([router_b[l], jnp.full((LANES - N_EXPERTS,), -jnp.inf, F32)]).reshape(1, LANES)
    gnw = gdn_norm_w[l].reshape(1, GDN_DV)
    final_w = final_norm_w.reshape(1, d)

    def mixer(x2d, b, t, aug):
        tm = _row_tile(t if aug else x2d.shape[0], 256)
        q, k, v, kb, vb, gqkv, gz, ma, mb, small = _proj(x2d, anw, w_r, sb, tm, t if aug else None)
        return dict(q=q.reshape(b, t, -1), k=k, v=v, kb=kb.reshape(b, t, -1), vb=vb.reshape(b, t, -1),
                    gqkv=gqkv.reshape(b, t, GDN_CONV_CH), gz=gz.reshape(b, t, GDN_VW), ma=ma, mb=mb,
                    small=small.reshape(b, t, SMALL_W))

    pp = mixer(x_prompt.reshape(n_p, d), bp, sp, True)
    logf_p = pp["small"][:, :, LOGF_LO:LOGF_LO + FOX_HEADS]
    fo_p = _attn_prompt(pp["q"], pp["kb"], pp["vb"], _row_tile(sp, ATTN_TQ), _row_tile(sp, ATTN_TK))
    rows_p = _row_tile(sp, 4 * CHUNK)
    go_p, state_p = _gdn(pp["gqkv"], pp["gz"], pp["small"], jnp.zeros((bp, CONV_W - 1, GDN_CONV_CH), F32),
                         jnp.zeros((bp, GDN_HEADS, GDN_DK, GDN_DV), F32), gdn_conv_w[l], gnw, rows_p)
    conv_p = pp["gqkv"][:, sp - (CONV_W - 1):, :]

    ps = mixer(x_sample.reshape(n_s, d), bs, ts, False)
    logf_s = ps["small"][:, :, LOGF_LO:LOGF_LO + FOX_HEADS]
    tot = past + ts
    tot_pad = -(-tot // LANES) * LANES
    logf_all = jnp.concatenate([cache_fox_logf[l].astype(F32), logf_s,
                                jnp.zeros((bs, tot_pad - tot, FOX_HEADS), F32)], axis=1)
    c_s = _seq_cumsum(jnp.transpose(logf_all, (0, 2, 1)))
    fo_s = _attn_step(ps["q"], ps["kb"], ps["vb"], jnp.transpose(cache_fox_k[l], (0, 2, 3, 1)),
                      jnp.transpose(cache_fox_v[l], (0, 2, 3, 1)), c_s)
    t_pad = -(-ts // CHUNK) * CHUNK
    padt = lambda a: jnp.pad(a, ((0, 0), (0, t_pad - ts), (0, 0)))
    go_s, state_s = _gdn(padt(ps["gqkv"]), padt(ps["gz"]), padt(ps["small"]), state_gdn_conv[l], state_gdn[l],
                         gdn_conv_w[l], gnw, CHUNK)
    go_s = go_s[:, :ts, :]
    conv_s = ps["gqkv"][:, ts - (CONV_W - 1):, :]

    tm_p = _row_tile(n_p, 512)
    tm_s = _row_tile(n_s, 512)
    x2_p, h2_p, gates_p, ir_p, cnt_p = _post(fo_p.reshape(n_p, FOX_W), go_p.reshape(n_p, GDN_VW), pp["ma"], pp["mb"],
                                             x_prompt.reshape(n_p, d), wfo, wgo, wout, gate_bias[l], fnw, rw, rb,
                                             jnp.zeros((1, LANES), F32), tm_p)
    x2_s, h2_s, gates_s, ir_s, cnt_s = _post(fo_s.reshape(n_s, FOX_W), go_s.reshape(n_s, GDN_VW), ps["ma"], ps["mb"],
                                             x_sample.reshape(n_s, d), wfo, wgo, wout, gate_bias[l], fnw, rw, rb,
                                             cnt_p, tm_s)

    tb = EXPERT_BLOCK
    nk = (n_p + n_s) * TOP_K
    n_blocks = -(-nk // tb) + N_EXPERTS
    counts = cnt_s[0, :N_EXPERTS].astype(jnp.int32)
    padded = (counts + tb - 1) // tb * tb
    pad_end = jnp.cumsum(padded)
    pad_start = pad_end - padded
    block_pos = jnp.arange(n_blocks, dtype=jnp.int32) * tb
    block_e = jnp.minimum(jnp.sum((pad_end[None, :] <= block_pos[:, None]).astype(jnp.int32), axis=1), N_EXPERTS - 1)
    n_used = (pad_end[-1:] // tb).astype(jnp.int32)
    experts = jnp.arange(N_EXPERTS, dtype=jnp.int32)[:, None, None]

    def dest(ir):
        start = jnp.sum(jnp.where(ir[None, :TOP_K] == experts, pad_start[:, None, None], 0), axis=0)
        return jnp.transpose(start + ir[TOP_K:]).reshape(-1)
    dest_p, dest_s = dest(ir_p), dest(ir_s)

    pad_first = pad_start + counts
    pad_aligned = (pad_first + ROW_GROUP - 1) // ROW_GROUP * ROW_GROUP
    rows = n_blocks * tb
    pad_tab = jnp.concatenate([pad_first, pad_aligned - pad_first, pad_aligned, (pad_end - pad_aligned) // ROW_GROUP,
                               pad_end[-1:], (rows - pad_end[-1:]) // ZERO_ROWS]).astype(jnp.int32)
    tm_d = _row_tile(n_s, 512)
    assert n_p % tm_d == 0 and tb % ZERO_ROWS == 0
    xs = _dispatch(jnp.concatenate([dest_p, dest_s]), h2_p, h2_s, pad_tab, rows, tm_d)
    ys = _experts(block_e, n_used, xs, expert_w_gu[l], expert_b_gu[l], expert_w_down[l], expert_b_down[l])
    y_p = _combine(dest_p, x2_p, gates_p, final_w, ys, tm_p)
    y_s = _combine(dest_s, x2_s, gates_s, final_w, ys, tm_s)

    hd = (FOX_HEADS, FOX_HEAD_DIM)
    seq_minor = lambda a: jnp.transpose(a.reshape(1, bp, *hd, sp), (0, 1, 4, 2, 3))
    return (y_p.reshape(bp, sp, d), y_s.reshape(bs, ts, d),
            seq_minor(pp["k"]), seq_minor(pp["v"]), logf_p[None],
            state_p[None], conv_p[None],
            ps["k"].reshape(1, bs, ts, *hd), ps["v"].reshape(1, bs, ts, *hd), logf_s[None],
            state_s[None], conv_s[None])
```

```python
import functools

import numpy as np
import jax
import jax.numpy as jnp
from jax import lax
from jax.experimental import pallas as pl
from jax.experimental.pallas import tpu as pltpu

F32 = jnp.float32
BF16 = jnp.bfloat16
HIGHEST = lax.Precision.HIGHEST

D_MODEL = 1024
FOX_HEADS = 8
FOX_HEAD_DIM = 64
FOX_W = FOX_HEADS * FOX_HEAD_DIM
FOX_SCALE = FOX_HEAD_DIM ** -0.5
GDN_HEADS = 4
GDN_DK = 128
GDN_DV = 128
GDN_KW = GDN_HEADS * GDN_DK
GDN_VW = GDN_HEADS * GDN_DV
GDN_CONV_CH = 2 * GDN_KW + GDN_VW
CONV_W = 4
CHUNK = 64
N_EXPERTS = 32
TOP_K = 4
D_FF = D_MODEL
SWIGLU_LIMIT = 7.0
SWIGLU_ALPHA = 1.702
RMS_EPS = 1e-6
NEG_INF = -1e30

LANES = 128
SMALL_W = LANES
LOGF_LO, G_LO, BETA_LO = 0, FOX_HEADS, FOX_HEADS + GDN_HEADS
C_Q, C_K, C_V = 0, FOX_W, 2 * FOX_W
C_GQKV = 3 * FOX_W
C_GZ = C_GQKV + GDN_CONV_CH
C_MA = C_GZ + GDN_VW
C_MB = C_MA + D_MODEL
C_SMALL = C_MB + D_MODEL
W_R = C_SMALL + SMALL_W

VMEM_LIMIT = 56 * 1024 * 1024
ATTN_TQ = 512
ATTN_TK = 512
ATTN_HEADS_PER_LOOP = 4
ATTN_DIAG_KEYS = 256
ROW_GROUP = 8
DMA_ROWS_PER_TRIP = 8
EXPERT_BLOCK = 512
PAD_GROUP_BITS = (EXPERT_BLOCK // ROW_GROUP).bit_length() - 1
ZERO_ROWS = ROW_GROUP << (PAD_GROUP_BITS - 1)
FF_CHUNK = 512


def _cparams(sem, vmem=VMEM_LIMIT):
    return pltpu.CompilerParams(dimension_semantics=sem, vmem_limit_bytes=vmem)


def _softplus(x):
    return jnp.maximum(x, 0.0) + jnp.log1p(jnp.exp(-jnp.abs(x)))


def _sigmoid(x):
    return jax.nn.sigmoid(x)


def _dot(a, b, precision=None):
    return jnp.dot(a, b, preferred_element_type=F32, precision=precision)


def _dot_nt(a, b, precision=None):
    return lax.dot_general(a, b, (((1,), (1,)), ((), ())), preferred_element_type=F32, precision=precision)


def _dot_tn(a, b, precision=None):
    return lax.dot_general(a, b, (((0,), (0,)), ((), ())), preferred_element_type=F32, precision=precision)


def _head_tiles(x):
    tiles = []
    for pair in range(FOX_HEADS // 2):
        t = x[:, pair * LANES:(pair + 1) * LANES]
        tiles += [t, pltpu.roll(t, FOX_HEAD_DIM, axis=1)]
    return tiles


def _proj_kernel(x_ref, nw_ref, w_ref, sb_ref, *rest, aug, tiles_per_seq):
    if aug:
        tri_ref, place_ref, q_ref, k_ref, v_ref, kb_ref, vb_ref, gqkv_ref, gz_ref, ma_ref, mb_ref, small_ref, carry = rest
    else:
        q_ref, k_ref, v_ref, kb_ref, vb_ref, gqkv_ref, gz_ref, ma_ref, mb_ref, small_ref = rest
    x = x_ref[...]
    ms = jnp.mean(x * x, axis=-1, keepdims=True)
    h = ((x * lax.rsqrt(ms + RMS_EPS)) * nw_ref[...]).astype(BF16)

    def sec(lo, width):
        return _dot(h, w_ref[:, lo:lo + width])

    q = sec(C_Q, FOX_W) * FOX_SCALE
    k = sec(C_K, FOX_W)
    v = sec(C_V, FOX_W)
    if aug:
        k_ref[...] = jnp.transpose(k)
        v_ref[...] = jnp.transpose(v)
    else:
        k_ref[...] = k
        v_ref[...] = v
    gqkv_ref[...] = sec(C_GQKV, GDN_CONV_CH)
    gz_ref[...] = sec(C_GZ, GDN_VW)
    ma_ref[...] = sec(C_MA, D_MODEL)
    mb_ref[...] = sec(C_MB, D_MODEL)
    z = sec(C_SMALL, SMALL_W) + sb_ref[0:1, :]
    lane = lax.broadcasted_iota(jnp.int32, z.shape, 1)
    logf = -_softplus(-z)
    g = -jnp.exp(sb_ref[1:2, :]) * _softplus(z)
    beta = _sigmoid(z)
    small_ref[...] = jnp.where(lane < G_LO, logf, jnp.where(lane < BETA_LO, g, beta))
    if not aug:
        q_ref[...] = q.astype(BF16)
        kb_ref[...] = k.astype(BF16)
        vb_ref[...] = v.astype(BF16)
        return

    @pl.when(pl.program_id(0) % tiles_per_seq == 0)
    def _():
        carry[...] = jnp.zeros(carry.shape, F32)

    lf = jnp.where(lane < G_LO, logf, 0.0)
    l1 = lf.astype(BF16)
    r1 = lf - l1.astype(F32)
    l2 = r1.astype(BF16)
    l3 = (r1 - l2.astype(F32)).astype(BF16)
    tri = tri_ref[...]
    c = (_dot(tri, l1) + (_dot(tri, l2) + _dot(tri, l3))) + carry[...]
    carry[...] = c[c.shape[0] - 1:, :]
    nc = -c
    p1 = nc.astype(BF16).astype(F32)
    r1 = nc - p1
    p2 = r1.astype(BF16).astype(F32)
    p3 = (r1 - p2).astype(BF16).astype(F32)
    parts = p1 + pltpu.roll(p2, FOX_HEADS, axis=1) + pltpu.roll(p3, 2 * FOX_HEADS, axis=1)
    ext = _dot(parts.astype(BF16), place_ref[...])
    low = lane < FOX_HEAD_DIM
    ones3 = jnp.where((lane >= FOX_HEAD_DIM) & (lane < FOX_HEAD_DIM + 3), 1.0, 0.0)
    one1 = jnp.where(lane == FOX_HEAD_DIM, 1.0, 0.0)
    for hd, (qt, kt, vt) in enumerate(zip(_head_tiles(q), _head_tiles(k), _head_tiles(v))):
        cols = slice(hd * LANES, (hd + 1) * LANES)
        q_ref[:, cols] = jnp.where(low, qt, ones3).astype(BF16)
        kb_ref[:, cols] = jnp.where(low, kt, ext[:, cols]).astype(BF16)
        vb_ref[:, cols] = jnp.where(low, vt, one1).astype(BF16)


def _proj(x2d, nw, w_r, sb, tm, seq=None):
    n = x2d.shape[0]
    aug = seq is not None
    row = lambda w: pl.BlockSpec((tm, w), lambda i: (i, 0))
    const = lambda shape: pl.BlockSpec(shape, lambda i: (0, 0))
    aw = FOX_HEADS * LANES if aug else FOX_W
    outs = [(aw, BF16), (FOX_W, F32), (FOX_W, F32), (aw, BF16), (aw, BF16), (GDN_CONV_CH, F32),
            (GDN_VW, F32), (D_MODEL, F32), (D_MODEL, F32), (SMALL_W, F32)]
    in_specs = [row(D_MODEL), const((1, D_MODEL)), const((D_MODEL, W_R)), const((8, SMALL_W))]
    args = [x2d, nw, w_r, sb]
    scratch = []
    tps = seq // tm if aug else 1
    head_spec = pl.BlockSpec((None, FOX_W, tm), lambda i: (i // tps, 0, i % tps))
    head_shape = jax.ShapeDtypeStruct((n // seq if aug else 1, FOX_W, seq if aug else tm), F32)
    if aug:
        place = np.zeros((LANES, FOX_HEADS * LANES), np.float32)
        for hd in range(FOX_HEADS):
            for j in range(3):
                place[j * FOX_HEADS + hd, hd * LANES + FOX_HEAD_DIM + j] = 1.0
        in_specs += [const((tm, tm)), const((LANES, FOX_HEADS * LANES))]
        args += [jnp.asarray(np.tril(np.ones((tm, tm), np.float32)), BF16), jnp.asarray(place, BF16)]
        scratch = [pltpu.VMEM((1, SMALL_W), F32)]
    return pl.pallas_call(
        functools.partial(_proj_kernel, aug=aug, tiles_per_seq=(seq // tm if aug else 1)),
        grid=(n // tm,),
        in_specs=in_specs,
        out_specs=[head_spec if aug and idx in (1, 2) else row(w) for idx, (w, _) in enumerate(outs)],
        out_shape=[head_shape if aug and idx in (1, 2) else jax.ShapeDtypeStruct((n, w), dt)
                   for idx, (w, dt) in enumerate(outs)],
        scratch_shapes=scratch,
        compiler_params=_cparams(("arbitrary",)),
        name="proj",
    )(*args)


def _cumsum_kernel(x_ref, tri_ref, lmat_ref, c_ref):
    x = x_ref[...]
    cb = _dot(x, tri_ref[...], HIGHEST)
    tot = jnp.broadcast_to(cb[:, LANES - 1:LANES], cb.shape)
    c_ref[...] = cb + _dot(lmat_ref[...], tot, HIGHEST)


def _seq_cumsum(logf_bhs):
    b, h, s = logf_bhs.shape
    nb = s // LANES
    r = h * nb
    tri = jnp.asarray(np.triu(np.ones((LANES, LANES), np.float32)))
    rr = np.arange(r)
    lmat = jnp.asarray(((rr[:, None] // nb == rr[None, :] // nb) & (rr[None, :] < rr[:, None])).astype(np.float32))
    out = pl.pallas_call(
        _cumsum_kernel,
        grid=(b,),
        in_specs=[pl.BlockSpec((None, r, LANES), lambda i: (i, 0, 0)),
                  pl.BlockSpec((LANES, LANES), lambda i: (0, 0)),
                  pl.BlockSpec((r, r), lambda i: (0, 0))],
        out_specs=pl.BlockSpec((None, r, LANES), lambda i: (i, 0, 0)),
        out_shape=jax.ShapeDtypeStruct((b, r, LANES), F32),
        compiler_params=_cparams(("parallel",)),
        name="logf_cumsum",
    )(logf_bhs.reshape(b, r, LANES), tri, lmat)
    return out.reshape(b, h, s)


def _attn_prompt_kernel(q_ref, k_ref, v_ref, o_ref, *, tq, tk):
    i = pl.program_id(1)
    q0 = pl.multiple_of(i * tq, tq)
    n_full = (i * tq) // tk
    lane = lax.broadcasted_iota(jnp.int32, (1, LANES), 1)
    sub = ATTN_DIAG_KEYS if tq % ATTN_DIAG_KEYS == 0 else tq
    outs = []
    for g0 in range(0, FOX_HEADS, ATTN_HEADS_PER_LOOP):
        heads = list(range(g0, g0 + ATTN_HEADS_PER_LOOP))

        def step(k0, width, row0, carry, mask, heads=heads):
            rows = tq - row0
            if mask:
                keep = (lax.broadcasted_iota(jnp.int32, (rows, width), 0) + (q0 + row0)
                        >= lax.broadcasted_iota(jnp.int32, (rows, width), 1) + k0)
            ss = [_dot_nt(q_ref[row0:, h * LANES:(h + 1) * LANES], k_ref[pl.ds(k0, width), h * LANES:(h + 1) * LANES])
                  for h in heads]
            out = []
            for (m, acc), s, h in zip(carry, ss, heads):
                if mask:
                    s = jnp.where(keep, s, NEG_INF)
                m_old, acc_old = m[row0:], acc[row0:]
                m_new = jnp.maximum(m_old, jnp.max(s, axis=-1, keepdims=True))
                p = jnp.exp(s - m_new).astype(BF16)
                acc_new = jnp.exp(m_old - m_new) * acc_old + _dot(p, v_ref[pl.ds(k0, width), h * LANES:(h + 1) * LANES])
                if row0:
                    m_new = jnp.concatenate([m[:row0], m_new], axis=0)
                    acc_new = jnp.concatenate([acc[:row0], acc_new], axis=0)
                out.append((m_new, acc_new))
            return tuple(out)

        init = tuple((jnp.full((tq, 1), NEG_INF, F32), jnp.zeros((tq, LANES), F32)) for _ in heads)
        carry = lax.fori_loop(0, n_full, lambda j, cr, step=step: step(pl.multiple_of(j * tk, tk), tk, 0, cr, False),
                              init)
        for jj in range(tq // sub):
            carry = step(pl.multiple_of(q0 + jj * sub, sub), sub, jj * sub, carry, True)
        for _, acc in carry:
            outs.append(acc * (1.0 / acc[:, FOX_HEAD_DIM:FOX_HEAD_DIM + 1]))
    for pair in range(FOX_HEADS // 2):
        o_ref[:, pair * LANES:(pair + 1) * LANES] = jnp.where(
            lane < FOX_HEAD_DIM, outs[2 * pair], pltpu.roll(outs[2 * pair + 1], FOX_HEAD_DIM, axis=1)).astype(BF16)


def _attn_prompt(qa, ka, va, tq, tk):
    b, s, aw = qa.shape
    assert tq % tk == 0, "query blocks must start on a key-tile boundary"
    return pl.pallas_call(
        functools.partial(_attn_prompt_kernel, tq=tq, tk=tk),
        grid=(b, s // tq),
        in_specs=[pl.BlockSpec((None, tq, aw), lambda bi, i: (bi, i, 0)),
                  pl.BlockSpec((None, s, aw), lambda bi, i: (bi, 0, 0)),
                  pl.BlockSpec((None, s, aw), lambda bi, i: (bi, 0, 0))],
        out_specs=pl.BlockSpec((None, tq, FOX_W), lambda bi, i: (bi, i, 0)),
        out_shape=jax.ShapeDtypeStruct((b, s, FOX_W), BF16),
        compiler_params=_cparams(("parallel", "arbitrary")),
        name="attn_prompt",
    )(qa, ka, va)


def _attn_step_kernel(q_ref, kn_ref, vn_ref, kct_ref, vct_ref, c_ref, o_ref, *, past, t):
    rowi = lax.broadcasted_iota(jnp.int32, (t, t), 0)
    coli = lax.broadcasted_iota(jnp.int32, (t, t), 1)
    causal = rowi >= coli
    for h in range(FOX_HEADS):
        cols = slice(h * FOX_HEAD_DIM, (h + 1) * FOX_HEAD_DIM)
        qh = q_ref[:, cols]
        c_c = c_ref[h:h + 1, 0:past]
        c_n = c_ref[h:h + 1, past:past + t]
        c0 = c_n[:, 0:1]
        s_c = _dot(qh, kct_ref[h].astype(BF16)) + (c0 - c_c)
        s_n = jnp.where(causal, _dot_nt(qh, kn_ref[:, cols]) + (c0 - c_n), NEG_INF)
        m = jnp.maximum(jnp.max(s_c, axis=-1, keepdims=True), jnp.max(s_n, axis=-1, keepdims=True))
        p_c = jnp.exp(s_c - m)
        p_n = jnp.exp(s_n - m)
        l = jnp.sum(p_c, axis=-1, keepdims=True) + jnp.sum(p_n, axis=-1, keepdims=True)
        acc = _dot_nt(p_c.astype(BF16), vct_ref[h].astype(BF16)) + _dot(p_n.astype(BF16), vn_ref[:, cols])
        o_ref[:, cols] = (acc * (1.0 / l)).astype(BF16)


def _attn_step(qb, knb, vnb, kc_t, vc_t, c_all):
    b, t, _ = qb.shape
    past = kc_t.shape[-1]
    sp = c_all.shape[-1]
    new = pl.BlockSpec((None, t, FOX_W), lambda bi: (bi, 0, 0))
    old = pl.BlockSpec((None, FOX_HEADS, FOX_HEAD_DIM, past), lambda bi: (bi, 0, 0, 0))
    return pl.pallas_call(
        functools.partial(_attn_step_kernel, past=past, t=t),
        grid=(b,),
        in_specs=[new, new, new, old, old, pl.BlockSpec((None, FOX_HEADS, sp), lambda bi: (bi, 0, 0))],
        out_specs=new,
        out_shape=jax.ShapeDtypeStruct((b, t, FOX_W), BF16),
        compiler_params=_cparams(("parallel",)),
        name="attn_step",
    )(qb, knb, vnb, kc_t, vc_t, c_all)


def _split(a):
    hi = a.astype(BF16)
    return hi, (a - hi.astype(F32)).astype(BF16)


def _dot3(a_parts, b_parts):
    (ah, al), (bh, bl) = a_parts, b_parts
    return _dot(ah, bh) + (_dot(ah, bl) + _dot(al, bh))


def _gdn_kernel(x_ref, z_ref, sm_ref, cb_ref, s0_ref, cw_ref, nw_ref, o_ref, s_ref, xbuf, *, rows):
    c = pl.program_id(1)
    L = CHUNK

    @pl.when(c == 0)
    def _():
        xbuf[8 - (CONV_W - 1):8, :] = cb_ref[...]
        s_ref[...] = s0_ref[...]

    xbuf[8:8 + rows, :] = x_ref[...]
    conv = xbuf[5:5 + rows, :] * cw_ref[0:1, :]
    for i in range(1, CONV_W):
        conv = conv + xbuf[5 + i:5 + i + rows, :] * cw_ref[i:i + 1, :]
    xbuf[5:8, :] = xbuf[rows + 5:rows + 8, :]
    conv = conv * _sigmoid(conv)

    ri = lax.broadcasted_iota(jnp.int32, (L, L), 0)
    ci = lax.broadcasted_iota(jnp.int32, (L, L), 1)
    tri_incl = ri >= ci
    tri_strict = ri > ci
    eye = ri == ci
    eye_f = eye.astype(F32)
    tril_b = tri_incl.astype(BF16)

    units = []
    for ch in range(rows // L):
        r0 = ch * L
        sm = sm_ref[r0:r0 + L, :]
        sh, sl = _split(sm)
        sl2 = (sm - sh.astype(F32) - sl.astype(F32)).astype(BF16)
        gcum = _dot(tril_b, sh) + (_dot(tril_b, sl) + _dot(tril_b, sl2))
        for h in range(GDN_HEADS):
            xq = conv[r0:r0 + L, h * GDN_DK:(h + 1) * GDN_DK]
            xk = conv[r0:r0 + L, GDN_KW + h * GDN_DK:GDN_KW + (h + 1) * GDN_DK]
            v = conv[r0:r0 + L, 2 * GDN_KW + h * GDN_DV:2 * GDN_KW + (h + 1) * GDN_DV]
            q = xq * lax.rsqrt(jnp.sum(xq * xq, axis=-1, keepdims=True) + RMS_EPS) * (GDN_DK ** -0.5)
            k = xk * lax.rsqrt(jnp.sum(xk * xk, axis=-1, keepdims=True) + RMS_EPS)
            beta = sm[:, BETA_LO + h:BETA_LO + h + 1]
            gc = gcum[:, G_LO + h:G_LO + h + 1]
            grow = jnp.sum(jnp.where(eye, gc, 0.0), axis=0, keepdims=True)
            decay = jnp.where(tri_incl, jnp.exp(jnp.where(tri_incl, gc - grow, 0.0)), 0.0)
            eg = jnp.exp(gc)
            glast = gc[L - 1:L, :]
            kb = k.astype(BF16)
            qb = q.astype(BF16)
            a = jnp.where(tri_strict, _dot_nt(kb, kb) * decay, 0.0) * beta
            units.append(dict(r0=r0, h=h, a=a, eg=eg, eglast=jnp.exp(glast), qb=qb,
                              rhs=jnp.concatenate([beta * v, (beta * eg) * k], axis=1),
                              qk=(_dot_nt(qb, kb) * decay).astype(BF16),
                              kd=(k * jnp.exp(glast - gc)).astype(BF16)))
    xs = [eye_f - u["a"] for u in units]
    pp = [_split(u["a"]) for u in units]
    m = 2
    while m < L:
        pp = [_split(_dot3(p2, p2)) for p2 in pp]
        xs = [x + _dot3(_split(x), p2) for x, p2 in zip(xs, pp)]
        m *= 2
    for u, x in zip(units, xs):
        uw = _dot3(_split(x), _split(u["rhs"]))
        u["u"] = uw[:, :GDN_DV]
        u["qw"] = jnp.concatenate([u["qb"], uw[:, GDN_DV:].astype(BF16)], axis=0)

    state = [s_ref[h] for h in range(GDN_HEADS)]
    for u in units:
        h, r0 = u["h"], u["r0"]
        s = state[h]
        r = _dot(u["qw"], s.astype(BF16))
        db = (u["u"] - r[L:, :]).astype(BF16)
        o = u["eg"] * r[:L, :] + _dot(u["qk"], db)
        state[h] = u["eglast"] * s + _dot_tn(u["kd"], db)
        on = o * lax.rsqrt(jnp.mean(o * o, axis=-1, keepdims=True) + RMS_EPS) * nw_ref[...]
        zz = z_ref[r0:r0 + L, h * GDN_DV:(h + 1) * GDN_DV]
        o_ref[r0:r0 + L, h * GDN_DV:(h + 1) * GDN_DV] = (on * (zz * _sigmoid(zz))).astype(BF16)
    for h in range(GDN_HEADS):
        s_ref[h] = state[h]


def _gdn(gqkv, gz, small, conv_buf, s0, conv_w, norm_w, rows):
    b, t, _ = gqkv.shape
    blk = lambda w: pl.BlockSpec((None, rows, w), lambda bi, c: (bi, c, 0))
    state = pl.BlockSpec((None, GDN_HEADS, GDN_DK, GDN_DV), lambda bi, c: (bi, 0, 0, 0))
    return pl.pallas_call(
        functools.partial(_gdn_kernel, rows=rows),
        grid=(b, t // rows),
        in_specs=[blk(GDN_CONV_CH), blk(GDN_VW), blk(SMALL_W),
                  pl.BlockSpec((None, CONV_W - 1, GDN_CONV_CH), lambda bi, c: (bi, 0, 0)),
                  state,
                  pl.BlockSpec((CONV_W, GDN_CONV_CH), lambda bi, c: (0, 0)),
                  pl.BlockSpec((1, GDN_DV), lambda bi, c: (0, 0))],
        out_specs=[blk(GDN_VW), state],
        out_shape=[jax.ShapeDtypeStruct((b, t, GDN_VW), BF16),
                   jax.ShapeDtypeStruct((b, GDN_HEADS, GDN_DK, GDN_DV), F32)],
        scratch_shapes=[pltpu.VMEM((rows + 8, GDN_CONV_CH), F32)],
        compiler_params=_cparams(("parallel", "arbitrary")),
        name="gdn",
    )(gqkv, gz, small, conv_buf, s0, conv_w, norm_w)


def _post_kernel(fo_ref, go_ref, ma_ref, mb_ref, x_ref, wfo_ref, wgo_ref, wout_ref, gb_ref, nw_ref, rw_ref, rb_ref,
                 tri_ref, cin_ref, x2_ref, h2_ref, gates_ref, ir_ref, cnt_ref, carry):
    i = pl.program_id(0)

    @pl.when(i == 0)
    def _():
        carry[...] = cin_ref[...]

    ya = _dot(fo_ref[...], wfo_ref[...])
    yb = _dot(go_ref[...], wgo_ref[...])
    merged = _sigmoid(ma_ref[...] + gb_ref[0:1, :]) * ya + _sigmoid(mb_ref[...] + gb_ref[1:2, :]) * yb
    x2 = x_ref[...] + _dot(merged.astype(BF16), wout_ref[...])
    x2_ref[...] = x2
    h2 = (x2 * lax.rsqrt(jnp.mean(x2 * x2, axis=-1, keepdims=True) + RMS_EPS)) * nw_ref[...]
    h2_ref[...] = h2
    logits = _dot(h2.astype(BF16), rw_ref[0]) + rb_ref[...]
    lane = lax.broadcasted_iota(jnp.int32, logits.shape, 1).astype(F32)
    work = logits
    vals, hits = [], []
    for _ in range(TOP_K):
        m = jnp.max(work, axis=-1, keepdims=True)
        idx = jnp.min(jnp.where(work == m, lane, float(LANES)), axis=-1, keepdims=True)
        hit = lane == idx
        vals.append(m)
        hits.append((hit, idx))
        work = jnp.where(hit, -jnp.inf, work)
    es = [jnp.exp(v - vals[0]) for v in vals]
    denom = es[0] + es[1] + es[2] + es[3]
    cnt = jnp.zeros(logits.shape, F32)
    for hit, _ in hits:
        cnt = cnt + hit.astype(F32)
    base = _dot(tri_ref[...], cnt.astype(BF16)) + carry[...]
    gates = jnp.zeros(logits.shape, F32)
    ir = jnp.zeros(logits.shape, F32)
    for kk, (hit, idx) in enumerate(hits):
        rank = jnp.sum(jnp.where(hit, base, 0.0), axis=-1, keepdims=True)
        gates = gates + jnp.where(lane == float(kk), es[kk] / denom, 0.0)
        ir = ir + jnp.where(lane == float(kk), idx, 0.0) + jnp.where(lane == float(TOP_K + kk), rank, 0.0)
    gates_ref[...] = gates
    ir_ref[...] = jnp.transpose(ir)[0:2 * TOP_K, :].astype(jnp.int32)
    carry[...] = carry[...] + jnp.sum(cnt, axis=0, keepdims=True)
    cnt_ref[...] = carry[...]


def _post(fo, go, ma, mb, x, wfo, wgo, wout, gate_bias, nw, rw, rb, cnt_in, tm):
    n = x.shape[0]
    row = lambda w: pl.BlockSpec((tm, w), lambda i: (i, 0))
    const = lambda shape: pl.BlockSpec(shape, lambda i: (0, 0))
    tri = jnp.asarray(np.tril(np.ones((tm, tm), np.float32), -1), BF16)
    return pl.pallas_call(
        _post_kernel,
        grid=(n // tm,),
        in_specs=[row(FOX_W), row(GDN_VW), row(D_MODEL), row(D_MODEL), row(D_MODEL),
                  const((FOX_W, D_MODEL)), const((GDN_VW, D_MODEL)), const((D_MODEL, D_MODEL)),
                  const((2, D_MODEL)), const((1, D_MODEL)),
                  pl.BlockSpec((2, D_MODEL, LANES), lambda i: (0, 0, 0)), const((1, LANES)),
                  const((tm, tm)), const((1, LANES))],
        out_specs=[row(D_MODEL), row(D_MODEL), row(LANES), pl.BlockSpec((2 * TOP_K, tm), lambda i: (0, i)),
                   const((1, LANES))],
        out_shape=[jax.ShapeDtypeStruct((n, D_MODEL), F32), jax.ShapeDtypeStruct((n, D_MODEL), F32),
                   jax.ShapeDtypeStruct((n, LANES), F32), jax.ShapeDtypeStruct((2 * TOP_K, n), jnp.int32),
                   jax.ShapeDtypeStruct((1, LANES), F32)],
        scratch_shapes=[pltpu.VMEM((1, LANES), F32)],
        compiler_params=_cparams(("arbitrary",)),
        name="post_router",
    )(fo, go, ma, mb, x, wfo, wgo, wout, gate_bias, nw, rw, rb, tri, cnt_in)


def _dispatch_kernel(pad_ref, dest_ref, hp_ref, hs_ref, xs_ref, sem, zbuf, zsem, *, tm, np_tiles):
    i = pl.program_id(0)
    zrows = zbuf.shape[0]

    @pl.when(i == 0)
    def _():
        zbuf[...] = jnp.zeros(zbuf.shape, zbuf.dtype)
        for phase in ("start", "wait"):
            def run(cp, phase=phase):
                cp.start() if phase == "start" else cp.wait()

            def per_expert(e, carry, run=run):
                first = pad_ref[e]
                pos = pad_ref[2 * N_EXPERTS + e]
                groups = pad_ref[3 * N_EXPERTS + e]

                def one(j, c2):
                    run(pltpu.make_async_copy(zbuf.at[pl.ds(0, 1), :], xs_ref.at[pl.ds(first + j, 1), :], zsem))
                    return c2

                lax.fori_loop(0, pad_ref[N_EXPERTS + e], one, 0)
                for b in reversed(range(PAD_GROUP_BITS)):
                    size = ROW_GROUP << b
                    bit = (groups >> b) & 1

                    @pl.when(bit == 1)
                    def _(pos=pos, size=size):
                        run(pltpu.make_async_copy(zbuf.at[pl.ds(0, size), :],
                                                  xs_ref.at[pl.ds(pl.multiple_of(pos, ROW_GROUP), size), :], zsem))

                    pos = pos + bit * size
                return carry

            lax.fori_loop(0, N_EXPERTS, per_expert, 0)

            def tail(j, carry, run=run):
                row = pl.multiple_of(pad_ref[4 * N_EXPERTS] + j * zrows, zrows)
                run(pltpu.make_async_copy(zbuf, xs_ref.at[pl.ds(row, zrows), :], zsem))
                return carry

            lax.fori_loop(0, pad_ref[4 * N_EXPERTS + 1], tail, 0)

    group = min(DMA_ROWS_PER_TRIP, tm)

    def scatter(h_ref):
        def issue(tg, carry):
            t0 = pl.multiple_of(tg * group, group)
            for r in range(group):
                for kk in range(TOP_K):
                    d = dest_ref[(t0 + r) * TOP_K + kk]
                    pltpu.make_async_copy(h_ref.at[pl.ds(t0 + r, 1), :], xs_ref.at[pl.ds(d, 1), :],
                                          sem).start(priority=kk % 2)
            return carry

        lax.fori_loop(0, tm // group, issue, 0)
        for kk in range(TOP_K):
            pltpu.make_async_copy(h_ref, xs_ref.at[pl.ds(0, tm), :], sem).wait()

    @pl.when(i < np_tiles)
    def _():
        scatter(hp_ref)

    @pl.when(i >= np_tiles)
    def _():
        scatter(hs_ref)


def _dispatch(dest_flat, h_p, h_s, pad_tab, rows, tm):
    np_tiles, ns_tiles = h_p.shape[0] // tm, h_s.shape[0] // tm
    return pl.pallas_call(
        functools.partial(_dispatch_kernel, tm=tm, np_tiles=np_tiles),
        grid_spec=pltpu.PrefetchScalarGridSpec(
            num_scalar_prefetch=1, grid=(np_tiles + ns_tiles,),
            in_specs=[pl.BlockSpec((tm * TOP_K,), lambda i, pad: (i,), memory_space=pltpu.SMEM),
                      pl.BlockSpec((tm, D_MODEL), lambda i, pad: (jnp.minimum(i, np_tiles - 1), 0)),
                      pl.BlockSpec((tm, D_MODEL), lambda i, pad: (jnp.maximum(i - np_tiles, 0), 0))],
            out_specs=pl.BlockSpec(memory_space=pl.ANY),
            scratch_shapes=[pltpu.SemaphoreType.DMA(()), pltpu.VMEM((ZERO_ROWS, D_MODEL), F32),
                            pltpu.SemaphoreType.DMA(())]),
        out_shape=jax.ShapeDtypeStruct((rows, D_MODEL), F32),
        compiler_params=_cparams(("arbitrary",)),
        name="moe_dispatch",
    )(pad_tab, dest_flat, h_p, h_s)


def _expert_kernel(be_ref, nu_ref, x_ref, wgu_ref, bgu_ref, wd_ref, bd_ref, y_ref, wgu_b, wd_b):
    i = pl.program_id(0)
    changed = jnp.logical_or(i == 0, be_ref[i] != be_ref[jnp.maximum(i - 1, 0)])

    @pl.when(changed)
    def _():
        for f in range(0, 2 * D_FF, FF_CHUNK):
            wgu_b[:, f:f + FF_CHUNK] = wgu_ref[:, f:f + FF_CHUNK].astype(BF16)
        for f in range(0, D_FF, FF_CHUNK):
            wd_b[f:f + FF_CHUNK, :] = wd_ref[f:f + FF_CHUNK, :].astype(BF16)

    @pl.when(i < nu_ref[0])
    def _():
        x = x_ref[...].astype(BF16)
        acc = jnp.zeros(y_ref.shape, F32)
        for f in range(0, D_FF, FF_CHUNK):
            gate = _dot(x, wgu_b[:, f:f + FF_CHUNK]) + bgu_ref[:, f:f + FF_CHUNK]
            up = _dot(x, wgu_b[:, D_FF + f:D_FF + f + FF_CHUNK]) + bgu_ref[:, D_FF + f:D_FF + f + FF_CHUNK]
            gate = jnp.minimum(gate, SWIGLU_LIMIT)
            up = jnp.clip(up, -SWIGLU_LIMIT, SWIGLU_LIMIT)
            act = (up + 1.0) * (gate * _sigmoid(SWIGLU_ALPHA * gate))
            acc = acc + _dot(act.astype(BF16), wd_b[f:f + FF_CHUNK, :])
        y_ref[...] = acc + bd_ref[...]

    @pl.when(i >= nu_ref[0])
    def _():
        y_ref[...] = jnp.zeros(y_ref.shape, F32)


def _experts(block_e, n_used, xs, w_gu, b_gu, w_down, b_down):
    rows = xs.shape[0]
    tb = EXPERT_BLOCK
    grid_spec = pltpu.PrefetchScalarGridSpec(
        num_scalar_prefetch=2,
        grid=(rows // tb,),
        in_specs=[pl.BlockSpec((tb, D_MODEL), lambda i, be, nu: (jnp.minimum(i, jnp.maximum(nu[0] - 1, 0)), 0)),
                  pl.BlockSpec((None, D_MODEL, 2 * D_FF), lambda i, be, nu: (be[i], 0, 0)),
                  pl.BlockSpec((None, 1, 2 * D_FF), lambda i, be, nu: (be[i], 0, 0)),
                  pl.BlockSpec((None, D_FF, D_MODEL), lambda i, be, nu: (be[i], 0, 0)),
                  pl.BlockSpec((None, 1, D_MODEL), lambda i, be, nu: (be[i], 0, 0))],
        out_specs=pl.BlockSpec((tb, D_MODEL), lambda i, be, nu: (i, 0)),
        scratch_shapes=[pltpu.VMEM((D_MODEL, 2 * D_FF), BF16), pltpu.VMEM((D_FF, D_MODEL), BF16)],
    )
    return pl.pallas_call(
        _expert_kernel,
        grid_spec=grid_spec,
        out_shape=jax.ShapeDtypeStruct((rows, D_MODEL), F32),
        compiler_params=_cparams(("arbitrary",)),
        name="moe_experts",
    )(block_e, n_used, xs, w_gu, b_gu.reshape(N_EXPERTS, 1, 2 * D_FF), w_down, b_down.reshape(N_EXPERTS, 1, D_MODEL))


def _combine_kernel(dest_ref, x2_ref, gates_ref, fw_ref, ys_ref, y_ref, buf, sem, *, tm):
    group = min(DMA_ROWS_PER_TRIP, tm)

    def issue(tg, carry):
        t0 = pl.multiple_of(tg * group, group)
        for r in range(group):
            for kk in range(TOP_K):
                d = dest_ref[(t0 + r) * TOP_K + kk]
                pltpu.make_async_copy(ys_ref.at[pl.ds(d, 1), :], buf.at[kk, pl.ds(t0 + r, 1), :], sem).start(priority=kk % 2)
        return carry

    lax.fori_loop(0, tm // group, issue, 0)
    for kk in range(TOP_K):
        pltpu.make_async_copy(ys_ref.at[pl.ds(0, tm), :], buf.at[kk], sem).wait()
    gates = gates_ref[...]
    out = x2_ref[...]
    for kk in range(TOP_K):
        out = out + gates[:, kk:kk + 1] * buf[kk]
    y_ref[...] = (out * lax.rsqrt(jnp.mean(out * out, axis=-1, keepdims=True) + RMS_EPS)) * fw_ref[...]


def _combine(dest_flat, x2, gates, final_w, ys, tm):
    n = x2.shape[0]
    return pl.pallas_call(
        functools.partial(_combine_kernel, tm=tm),
        grid=(n // tm,),
        in_specs=[pl.BlockSpec((tm * TOP_K,), lambda i: (i,), memory_space=pltpu.SMEM),
                  pl.BlockSpec((tm, D_MODEL), lambda i: (i, 0)),
                  pl.BlockSpec((tm, LANES), lambda i: (i, 0)),
                  pl.BlockSpec((1, D_MODEL), lambda i: (0, 0)),
                  pl.BlockSpec(memory_space=pl.ANY)],
        out_specs=pl.BlockSpec((tm, D_MODEL), lambda i: (i, 0)),
        out_shape=jax.ShapeDtypeStruct((n, D_MODEL), F32),
        scratch_shapes=[pltpu.VMEM((TOP_K, tm, D_MODEL), F32), pltpu.SemaphoreType.DMA(())],
        compiler_params=_cparams(("arbitrary",)),
        name="moe_combine",
    )(dest_flat, x2, gates, final_w, ys)


def _row_tile(n, want):
    t = min(want, n)
    while n % t:
        t //= 2
    return t


def kernel(x_prompt, x_sample, cache_fox_k, cache_fox_v, cache_fox_logf, state_gdn, state_gdn_conv, attn_norm_w, w_in, fox_f_bias, gdn_conv_w, gdn_a_log, gdn_dt_bias, gdn_norm_w, gate_bias, fox_w_o, gdn_w_o, w_out, ffn_norm_w, router_w, router_b, expert_w_gu, expert_b_gu, expert_w_down, expert_b_down, final_norm_w):
    l = 0
    bp, sp, d = x_prompt.shape
    bs, ts, _ = x_sample.shape
    past = cache_fox_k.shape[2]
    n_p, n_s = bp * sp, bs * ts

    w = w_in[l]
    o_ff = 3 * FOX_W
    o_gqkv = o_ff + FOX_HEADS
    o_gz = o_gqkv + GDN_CONV_CH
    o_ga = o_gz + GDN_VW
    o_gb = o_ga + GDN_HEADS
    o_ma = o_gb + GDN_HEADS
    w_r = jnp.concatenate(
        [w[:, :o_ff], w[:, o_gqkv:o_gz], w[:, o_gz:o_ga], w[:, o_ma:], w[:, o_ff:o_gqkv], w[:, o_ga:o_ma],
         jnp.zeros((d, SMALL_W - FOX_HEADS - 2 * GDN_HEADS), w.dtype)], axis=1).astype(BF16)
    sb = jnp.zeros((8, SMALL_W), F32)
    sb = sb.at[0, LOGF_LO:LOGF_LO + FOX_HEADS].set(fox_f_bias[l])
    sb = sb.at[0, G_LO:G_LO + GDN_HEADS].set(gdn_dt_bias[l])
    sb = sb.at[1, G_LO:G_LO + GDN_HEADS].set(gdn_a_log[l])
    anw = attn_norm_w[l].reshape(1, d)
    wfo = fox_w_o[l].astype(BF16)
    wgo = gdn_w_o[l].astype(BF16)
    wout = w_out[l].astype(BF16)
    fnw = ffn_norm_w[l].reshape(1, d)
    rw = jnp.concatenate([router_w[l], jnp.zeros((d, LANES - N_EXPERTS), F32)], axis=1)
    rw_hi = rw.astype(BF16)
    rw = jnp.stack([rw_hi, (rw - rw_hi.astype(F32)).astype(BF16)])
    rb = jnp.concatenate([router_b[l], jnp.full((LANES - N_EXPERTS,), -jnp.inf, F32)]).reshape(1, LANES)
    gnw = gdn_norm_w[l].reshape(1, GDN_DV)
    final_w = final_norm_w.reshape(1, d)

    def mixer(x2d, b, t, aug):
        tm = _row_tile(t if aug else x2d.shape[0], 256)
        q, k, v, kb, vb, gqkv, gz, ma, mb, small = _proj(x2d, anw, w_r, sb, tm, t if aug else None)
        return dict(q=q.reshape(b, t, -1), k=k, v=v, kb=kb.reshape(b, t, -1), vb=vb.reshape(b, t, -1),
                    gqkv=gqkv.reshape(b, t, GDN_CONV_CH), gz=gz.reshape(b, t, GDN_VW), ma=ma, mb=mb,
                    small=small.reshape(b, t, SMALL_W))

    pp = mixer(x_prompt.reshape(n_p, d), bp, sp, True)
    logf_p = pp["small"][:, :, LOGF_LO:LOGF_LO + FOX_HEADS]
    fo_p = _attn_prompt(pp["q"], pp["kb"], pp["vb"], _row_tile(sp, ATTN_TQ), _row_tile(sp, ATTN_TK))
    rows_p = _row_tile(sp, 4 * CHUNK)
    go_p, state_p = _gdn(pp["gqkv"], pp["gz"], pp["small"], jnp.zeros((bp, CONV_W - 1, GDN_CONV_CH), F32),
                         jnp.zeros((bp, GDN_HEADS, GDN_DK, GDN_DV), F32), gdn_conv_w[l], gnw, rows_p)
    conv_p = pp["gqkv"][:, sp - (CONV_W - 1):, :]

    ps = mixer(x_sample.reshape(n_s, d), bs, ts, False)
    logf_s = ps["small"][:, :, LOGF_LO:LOGF_LO + FOX_HEADS]
    tot = past + ts
    tot_pad = -(-tot // LANES) * LANES
    logf_all = jnp.concatenate([cache_fox_logf[l].astype(F32), logf_s,
                                jnp.zeros((bs, tot_pad - tot, FOX_HEADS), F32)], axis=1)
    c_s = _seq_cumsum(jnp.transpose(logf_all, (0, 2, 1)))
    fo_s = _attn_step(ps["q"], ps["kb"], ps["vb"], jnp.transpose(cache_fox_k[l], (0, 2, 3, 1)),
                      jnp.transpose(cache_fox_v[l], (0, 2, 3, 1)), c_s)
    t_pad = -(-ts // CHUNK) * CHUNK
    padt = lambda a: jnp.pad(a, ((0, 0), (0, t_pad - ts), (0, 0)))
    go_s, state_s = _gdn(padt(ps["gqkv"]), padt(ps["gz"]), padt(ps["small"]), state_gdn_conv[l], state_gdn[l],
                         gdn_conv_w[l], gnw, CHUNK)
    go_s = go_s[:, :ts, :]
    conv_s = ps["gqkv"][:, ts - (CONV_W - 1):, :]

    tm_p = _row_tile(n_p, 512)
    tm_s = _row_tile(n_s, 512)
    x2_p, h2_p, gates_p, ir_p, cnt_p = _post(fo_p.reshape(n_p, FOX_W), go_p.reshape(n_p, GDN_VW), pp["ma"], pp["mb"],
                                             x_prompt.reshape(n_p, d), wfo, wgo, wout, gate_bias[l], fnw, rw, rb,
                                             jnp.zeros((1, LANES), F32), tm_p)
    x2_s, h2_s, gates_s, ir_s, cnt_s = _post(fo_s.reshape(n_s, FOX_W), go_s.reshape(n_s, GDN_VW), ps["ma"], ps["mb"],
                                             x_sample.reshape(n_s, d), wfo, wgo, wout, gate_bias[l], fnw, rw, rb,
                                             cnt_p, tm_s)

    tb = EXPERT_BLOCK
    nk = (n_p + n_s) * TOP_K
    n_blocks = -(-nk // tb) + N_EXPERTS
    counts = cnt_s[0, :N_EXPERTS].astype(jnp.int32)
    padded = (counts + tb - 1) // tb * tb
    pad_end = jnp.cumsum(padded)
    pad_start = pad_end - padded
    block_pos = jnp.arange(n_blocks, dtype=jnp.int32) * tb
    block_e = jnp.minimum(jnp.sum((pad_end[None, :] <= block_pos[:, None]).astype(jnp.int32), axis=1), N_EXPERTS - 1)
    n_used = (pad_end[-1:] // tb).astype(jnp.int32)
    experts = jnp.arange(N_EXPERTS, dtype=jnp.int32)[:, None, None]

    def dest(ir):
        start = jnp.sum(jnp.where(ir[None, :TOP_K] == experts, pad_start[:, None, None], 0), axis=0)
        return jnp.transpose(start + ir[TOP_K:]).reshape(-1)
    dest_p, dest_s = dest(ir_p), dest(ir_s)

    pad_first = pad_start + counts
    pad_aligned = (pad_first + ROW_GROUP - 1) // ROW_GROUP * ROW_GROUP
    rows = n_blocks * tb
    pad_tab = jnp.concatenate([pad_first, pad_aligned - pad_first, pad_aligned, (pad_end - pad_aligned) // ROW_GROUP,
                               pad_end[-1:], (rows - pad_end[-1:]) // ZERO_ROWS]).astype(jnp.int32)
    tm_d = _row_tile(n_s, 512)
    assert n_p % tm_d == 0 and tb % ZERO_ROWS == 0
    xs = _dispatch(jnp.concatenate([dest_p, dest_s]), h2_p, h2_s, pad_tab, rows, tm_d)
    ys = _experts(block_e, n_used, xs, expert_w_gu[l], expert_b_gu[l], expert_w_down[l], expert_b_down[l])
    y_p = _combine(dest_p, x2_p, gates_p, final_w, ys, tm_p)
    y_s = _combine(dest_s, x2_s, gates_s, final_w, ys, tm_s)

    hd = (FOX_HEADS, FOX_HEAD_DIM)
    seq_minor = lambda a: jnp.transpose(a.reshape(1, bp, *hd, sp), (0, 1, 4, 2, 3))
    return (y_p.reshape(bp, sp, d), y_s.reshape(bs, ts, d),
            seq_minor(pp["k"]), seq_minor(pp["v"]), logf_p[None],
            state_p[None], conv_p[None],
            ps["k"].reshape(1, bs, ts, *hd), ps["v"].reshape(1, bs, ts, *hd), logf_s[None],
            state_s[None], conv_s[None])
```

```python
import functools

import numpy as np
import jax
import jax.numpy as jnp
from jax import lax
from jax.experimental import pallas as pl
from jax.experimental.pallas import tpu as pltpu

F32 = jnp.float32
BF16 = jnp.bfloat16
HIGHEST = lax.Precision.HIGHEST

D_MODEL = 1024
FOX_HEADS = 8
FOX_HEAD_DIM = 64
FOX_W = FOX_HEADS * FOX_HEAD_DIM
FOX_SCALE = FOX_HEAD_DIM ** -0.5
GDN_HEADS = 4
GDN_DK = 128
GDN_DV = 128
GDN_KW = GDN_HEADS * GDN_DK
GDN_VW = GDN_HEADS * GDN_DV
GDN_CONV_CH = 2 * GDN_KW + GDN_VW
CONV_W = 4
CHUNK = 64
N_EXPERTS = 32
TOP_K = 4
D_FF = D_MODEL
SWIGLU_LIMIT = 7.0
SWIGLU_ALPHA = 1.702
RMS_EPS = 1e-6
NEG_INF = -1e30

LANES = 128
SMALL_W = LANES
LOGF_LO, G_LO, BETA_LO = 0, FOX_HEADS, FOX_HEADS + GDN_HEADS
C_Q, C_K, C_V = 0, FOX_W, 2 * FOX_W
C_GQKV = 3 * FOX_W
C_GZ = C_GQKV + GDN_CONV_CH
C_MA = C_GZ + GDN_VW
C_MB = C_MA + D_MODEL
C_SMALL = C_MB + D_MODEL
W_R = C_SMALL + SMALL_W

VMEM_LIMIT = 56 * 1024 * 1024
ATTN_TQ = 512
ATTN_TK = 512
ATTN_HEADS_PER_LOOP = 4
ATTN_DIAG_KEYS = 256
GDN_SEQS_PER_STEP = 4
ROW_GROUP = 8
DMA_ROWS_PER_TRIP = 8
EXPERT_BLOCK = 512
PAD_GROUP_BITS = (EXPERT_BLOCK // ROW_GROUP).bit_length() - 1
ZERO_ROWS = ROW_GROUP << (PAD_GROUP_BITS - 1)
FF_CHUNK = 512


def _cparams(sem, vmem=VMEM_LIMIT):
    return pltpu.CompilerParams(dimension_semantics=sem, vmem_limit_bytes=vmem)


def _softplus(x):
    return jnp.maximum(x, 0.0) + jnp.log1p(jnp.exp(-jnp.abs(x)))


def _sigmoid(x):
    return jax.nn.sigmoid(x)


def _dot(a, b, precision=None):
    return jnp.dot(a, b, preferred_element_type=F32, precision=precision)


def _dot_nt(a, b, precision=None):
    return lax.dot_general(a, b, (((1,), (1,)), ((), ())), preferred_element_type=F32, precision=precision)


def _dot_tn(a, b, precision=None):
    return lax.dot_general(a, b, (((0,), (0,)), ((), ())), preferred_element_type=F32, precision=precision)


def _head_tiles(x):
    tiles = []
    for pair in range(FOX_HEADS // 2):
        t = x[:, pair * LANES:(pair + 1) * LANES]
        tiles += [t, pltpu.roll(t, FOX_HEAD_DIM, axis=1)]
    return tiles


def _proj_kernel(x_ref, nw_ref, w_ref, sb_ref, *rest, aug, tiles_per_seq):
    if aug:
        tri_ref, place_ref, q_ref, k_ref, v_ref, kb_ref, vb_ref, gqkv_ref, gz_ref, ma_ref, mb_ref, small_ref, carry = rest
    else:
        q_ref, k_ref, v_ref, kb_ref, vb_ref, gqkv_ref, gz_ref, ma_ref, mb_ref, small_ref = rest
    x = x_ref[...]
    ms = jnp.mean(x * x, axis=-1, keepdims=True)
    h = ((x * lax.rsqrt(ms + RMS_EPS)) * nw_ref[...]).astype(BF16)

    def sec(lo, width):
        return _dot(h, w_ref[:, lo:lo + width])

    q = sec(C_Q, FOX_W) * FOX_SCALE
    k = sec(C_K, FOX_W)
    v = sec(C_V, FOX_W)
    if aug:
        k_ref[...] = jnp.transpose(k)
        v_ref[...] = jnp.transpose(v)
    else:
        k_ref[...] = k
        v_ref[...] = v
    gqkv_ref[...] = sec(C_GQKV, GDN_CONV_CH)
    gz_ref[...] = sec(C_GZ, GDN_VW)
    ma_ref[...] = sec(C_MA, D_MODEL)
    mb_ref[...] = sec(C_MB, D_MODEL)
    z = sec(C_SMALL, SMALL_W) + sb_ref[0:1, :]
    lane = lax.broadcasted_iota(jnp.int32, z.shape, 1)
    logf = -_softplus(-z)
    g = -jnp.exp(sb_ref[1:2, :]) * _softplus(z)
    beta = _sigmoid(z)
    small_ref[...] = jnp.where(lane < G_LO, logf, jnp.where(lane < BETA_LO, g, beta))
    if not aug:
        q_ref[...] = q.astype(BF16)
        kb_ref[...] = k.astype(BF16)
        vb_ref[...] = v.astype(BF16)
        return

    @pl.when(pl.program_id(0) % tiles_per_seq == 0)
    def _():
        carry[...] = jnp.zeros(carry.shape, F32)

    lf = jnp.where(lane < G_LO, logf, 0.0)
    l1 = lf.astype(BF16)
    r1 = lf - l1.astype(F32)
    l2 = r1.astype(BF16)
    l3 = (r1 - l2.astype(F32)).astype(BF16)
    tri = tri_ref[...]
    c = (_dot(tri, l1) + (_dot(tri, l2) + _dot(tri, l3))) + carry[...]
    carry[...] = c[c.shape[0] - 1:, :]
    nc = -c
    p1 = nc.astype(BF16).astype(F32)
    r1 = nc - p1
    p2 = r1.astype(BF16).astype(F32)
    p3 = (r1 - p2).astype(BF16).astype(F32)
    parts = p1 + pltpu.roll(p2, FOX_HEADS, axis=1) + pltpu.roll(p3, 2 * FOX_HEADS, axis=1)
    ext = _dot(parts.astype(BF16), place_ref[...])
    low = lane < FOX_HEAD_DIM
    ones3 = jnp.where((lane >= FOX_HEAD_DIM) & (lane < FOX_HEAD_DIM + 3), 1.0, 0.0)
    one1 = jnp.where(lane == FOX_HEAD_DIM, 1.0, 0.0)
    for hd, (qt, kt, vt) in enumerate(zip(_head_tiles(q), _head_tiles(k), _head_tiles(v))):
        cols = slice(hd * LANES, (hd + 1) * LANES)
        q_ref[:, cols] = jnp.where(low, qt, ones3).astype(BF16)
        kb_ref[:, cols] = jnp.where(low, kt, ext[:, cols]).astype(BF16)
        vb_ref[:, cols] = jnp.where(low, vt, one1).astype(BF16)


def _proj(x2d, nw, w_r, sb, tm, seq=None):
    n = x2d.shape[0]
    aug = seq is not None
    row = lambda w: pl.BlockSpec((tm, w), lambda i: (i, 0))
    const = lambda shape: pl.BlockSpec(shape, lambda i: (0, 0))
    aw = FOX_HEADS * LANES if aug else FOX_W
    outs = [(aw, BF16), (FOX_W, F32), (FOX_W, F32), (aw, BF16), (aw, BF16), (GDN_CONV_CH, F32),
            (GDN_VW, F32), (D_MODEL, F32), (D_MODEL, F32), (SMALL_W, F32)]
    in_specs = [row(D_MODEL), const((1, D_MODEL)), const((D_MODEL, W_R)), const((8, SMALL_W))]
    args = [x2d, nw, w_r, sb]
    scratch = []
    tps = seq // tm if aug else 1
    head_spec = pl.BlockSpec((None, FOX_W, tm), lambda i: (i // tps, 0, i % tps))
    head_shape = jax.ShapeDtypeStruct((n // seq if aug else 1, FOX_W, seq if aug else tm), F32)
    if aug:
        place = np.zeros((LANES, FOX_HEADS * LANES), np.float32)
        for hd in range(FOX_HEADS):
            for j in range(3):
                place[j * FOX_HEADS + hd, hd * LANES + FOX_HEAD_DIM + j] = 1.0
        in_specs += [const((tm, tm)), const((LANES, FOX_HEADS * LANES))]
        args += [jnp.asarray(np.tril(np.ones((tm, tm), np.float32)), BF16), jnp.asarray(place, BF16)]
        scratch = [pltpu.VMEM((1, SMALL_W), F32)]
    return pl.pallas_call(
        functools.partial(_proj_kernel, aug=aug, tiles_per_seq=(seq // tm if aug else 1)),
        grid=(n // tm,),
        in_specs=in_specs,
        out_specs=[head_spec if aug and idx in (1, 2) else row(w) for idx, (w, _) in enumerate(outs)],
        out_shape=[head_shape if aug and idx in (1, 2) else jax.ShapeDtypeStruct((n, w), dt)
                   for idx, (w, dt) in enumerate(outs)],
        scratch_shapes=scratch,
        compiler_params=_cparams(("arbitrary",)),
        name="proj",
    )(*args)


def _cumsum_kernel(x_ref, tri_ref, lmat_ref, c_ref):
    x = x_ref[...]
    cb = _dot(x, tri_ref[...], HIGHEST)
    tot = jnp.broadcast_to(cb[:, LANES - 1:LANES], cb.shape)
    c_ref[...] = cb + _dot(lmat_ref[...], tot, HIGHEST)


def _seq_cumsum(logf_bhs):
    b, h, s = logf_bhs.shape
    nb = s // LANES
    r = h * nb
    tri = jnp.asarray(np.triu(np.ones((LANES, LANES), np.float32)))
    rr = np.arange(r)
    lmat = jnp.asarray(((rr[:, None] // nb == rr[None, :] // nb) & (rr[None, :] < rr[:, None])).astype(np.float32))
    out = pl.pallas_call(
        _cumsum_kernel,
        grid=(b,),
        in_specs=[pl.BlockSpec((None, r, LANES), lambda i: (i, 0, 0)),
                  pl.BlockSpec((LANES, LANES), lambda i: (0, 0)),
                  pl.BlockSpec((r, r), lambda i: (0, 0))],
        out_specs=pl.BlockSpec((None, r, LANES), lambda i: (i, 0, 0)),
        out_shape=jax.ShapeDtypeStruct((b, r, LANES), F32),
        compiler_params=_cparams(("parallel",)),
        name="logf_cumsum",
    )(logf_bhs.reshape(b, r, LANES), tri, lmat)
    return out.reshape(b, h, s)


def _attn_prompt_kernel(q_ref, k_ref, v_ref, o_ref, *, tq, tk):
    i = pl.program_id(1)
    q0 = pl.multiple_of(i * tq, tq)
    n_full = (i * tq) // tk
    lane = lax.broadcasted_iota(jnp.int32, (1, LANES), 1)
    sub = ATTN_DIAG_KEYS if tq % ATTN_DIAG_KEYS == 0 else tq
    outs = []
    for g0 in range(0, FOX_HEADS, ATTN_HEADS_PER_LOOP):
        heads = list(range(g0, g0 + ATTN_HEADS_PER_LOOP))

        def step(k0, width, row0, carry, mask, heads=heads):
            rows = tq - row0
            if mask:
                keep = (lax.broadcasted_iota(jnp.int32, (rows, width), 0) + (q0 + row0)
                        >= lax.broadcasted_iota(jnp.int32, (rows, width), 1) + k0)
            ss = [_dot_nt(q_ref[row0:, h * LANES:(h + 1) * LANES], k_ref[pl.ds(k0, width), h * LANES:(h + 1) * LANES])
                  for h in heads]
            out = []
            for (m, acc), s, h in zip(carry, ss, heads):
                if mask:
                    s = jnp.where(keep, s, NEG_INF)
                m_old, acc_old = m[row0:], acc[row0:]
                m_new = jnp.maximum(m_old, jnp.max(s, axis=-1, keepdims=True))
                p = jnp.exp(s - m_new).astype(BF16)
                acc_new = jnp.exp(m_old - m_new) * acc_old + _dot(p, v_ref[pl.ds(k0, width), h * LANES:(h + 1) * LANES])
                if row0:
                    m_new = jnp.concatenate([m[:row0], m_new], axis=0)
                    acc_new = jnp.concatenate([acc[:row0], acc_new], axis=0)
                out.append((m_new, acc_new))
            return tuple(out)

        init = tuple((jnp.full((tq, 1), NEG_INF, F32), jnp.zeros((tq, LANES), F32)) for _ in heads)
        carry = lax.fori_loop(0, n_full, lambda j, cr, step=step: step(pl.multiple_of(j * tk, tk), tk, 0, cr, False),
                              init)
        for jj in range(tq // sub):
            carry = step(pl.multiple_of(q0 + jj * sub, sub), sub, jj * sub, carry, True)
        for _, acc in carry:
            outs.append(acc * (1.0 / acc[:, FOX_HEAD_DIM:FOX_HEAD_DIM + 1]))
    for pair in range(FOX_HEADS // 2):
        o_ref[:, pair * LANES:(pair + 1) * LANES] = jnp.where(
            lane < FOX_HEAD_DIM, outs[2 * pair], pltpu.roll(outs[2 * pair + 1], FOX_HEAD_DIM, axis=1)).astype(BF16)


def _attn_prompt(qa, ka, va, tq, tk):
    b, s, aw = qa.shape
    assert tq % tk == 0, "query blocks must start on a key-tile boundary"
    return pl.pallas_call(
        functools.partial(_attn_prompt_kernel, tq=tq, tk=tk),
        grid=(b, s // tq),
        in_specs=[pl.BlockSpec((None, tq, aw), lambda bi, i: (bi, i, 0)),
                  pl.BlockSpec((None, s, aw), lambda bi, i: (bi, 0, 0)),
                  pl.BlockSpec((None, s, aw), lambda bi, i: (bi, 0, 0))],
        out_specs=pl.BlockSpec((None, tq, FOX_W), lambda bi, i: (bi, i, 0)),
        out_shape=jax.ShapeDtypeStruct((b, s, FOX_W), BF16),
        compiler_params=_cparams(("parallel", "arbitrary")),
        name="attn_prompt",
    )(qa, ka, va)


def _attn_step_kernel(q_ref, kn_ref, vn_ref, kct_ref, vct_ref, c_ref, o_ref, *, past, t):
    rowi = lax.broadcasted_iota(jnp.int32, (t, t), 0)
    coli = lax.broadcasted_iota(jnp.int32, (t, t), 1)
    causal = rowi >= coli
    for h in range(FOX_HEADS):
        cols = slice(h * FOX_HEAD_DIM, (h + 1) * FOX_HEAD_DIM)
        qh = q_ref[:, cols]
        c_c = c_ref[h:h + 1, 0:past]
        c_n = c_ref[h:h + 1, past:past + t]
        c0 = c_n[:, 0:1]
        s_c = _dot(qh, kct_ref[h].astype(BF16)) + (c0 - c_c)
        s_n = jnp.where(causal, _dot_nt(qh, kn_ref[:, cols]) + (c0 - c_n), NEG_INF)
        m = jnp.maximum(jnp.max(s_c, axis=-1, keepdims=True), jnp.max(s_n, axis=-1, keepdims=True))
        p_c = jnp.exp(s_c - m)
        p_n = jnp.exp(s_n - m)
        l = jnp.sum(p_c, axis=-1, keepdims=True) + jnp.sum(p_n, axis=-1, keepdims=True)
        acc = _dot_nt(p_c.astype(BF16), vct_ref[h].astype(BF16)) + _dot(p_n.astype(BF16), vn_ref[:, cols])
        o_ref[:, cols] = (acc * (1.0 / l)).astype(BF16)


def _attn_step(qb, knb, vnb, kc_t, vc_t, c_all):
    b, t, _ = qb.shape
    past = kc_t.shape[-1]
    sp = c_all.shape[-1]
    new = pl.BlockSpec((None, t, FOX_W), lambda bi: (bi, 0, 0))
    old = pl.BlockSpec((None, FOX_HEADS, FOX_HEAD_DIM, past), lambda bi: (bi, 0, 0, 0))
    return pl.pallas_call(
        functools.partial(_attn_step_kernel, past=past, t=t),
        grid=(b,),
        in_specs=[new, new, new, old, old, pl.BlockSpec((None, FOX_HEADS, sp), lambda bi: (bi, 0, 0))],
        out_specs=new,
        out_shape=jax.ShapeDtypeStruct((b, t, FOX_W), BF16),
        compiler_params=_cparams(("parallel",)),
        name="attn_step",
    )(qb, knb, vnb, kc_t, vc_t, c_all)


def _split(a):
    hi = a.astype(BF16)
    return hi, (a - hi.astype(F32)).astype(BF16)


def _dot3(a_parts, b_parts):
    (ah, al), (bh, bl) = a_parts, b_parts
    return _dot(ah, bh) + (_dot(ah, bl) + _dot(al, bh))


def _gdn_kernel(x_ref, z_ref, sm_ref, cb_ref, s0_ref, cw_ref, nw_ref, o_ref, s_ref, xbuf, *, rows, nb):
    c = pl.program_id(1)
    L = CHUNK
    hist = ROW_GROUP - (CONV_W - 1)

    @pl.when(c == 0)
    def _():
        xbuf[:, hist:ROW_GROUP, :] = cb_ref[...]
        s_ref[...] = s0_ref[...]

    xbuf[:, ROW_GROUP:ROW_GROUP + rows, :] = x_ref[...]
    conv = xbuf[:, hist:hist + rows, :] * cw_ref[0:1, :]
    for i in range(1, CONV_W):
        conv = conv + xbuf[:, hist + i:hist + i + rows, :] * cw_ref[i:i + 1, :]
    xbuf[:, hist:ROW_GROUP, :] = xbuf[:, rows + hist:rows + ROW_GROUP, :]
    conv = conv * _sigmoid(conv)

    ri = lax.broadcasted_iota(jnp.int32, (L, L), 0)
    ci = lax.broadcasted_iota(jnp.int32, (L, L), 1)
    tri_incl = ri >= ci
    tri_strict = ri > ci
    eye = ri == ci
    eye_f = eye.astype(F32)
    tril_b = tri_incl.astype(BF16)

    units = []
    for bc in range(nb * (rows // L)):
        bi, ch = divmod(bc, rows // L)
        r0 = ch * L
        sm = sm_ref[bi, r0:r0 + L, :]
        sh, sl = _split(sm)
        sl2 = (sm - sh.astype(F32) - sl.astype(F32)).astype(BF16)
        gcum = _dot(tril_b, sh) + (_dot(tril_b, sl) + _dot(tril_b, sl2))
        for h in range(GDN_HEADS):
            xq = conv[bi, r0:r0 + L, h * GDN_DK:(h + 1) * GDN_DK]
            xk = conv[bi, r0:r0 + L, GDN_KW + h * GDN_DK:GDN_KW + (h + 1) * GDN_DK]
            v = conv[bi, r0:r0 + L, 2 * GDN_KW + h * GDN_DV:2 * GDN_KW + (h + 1) * GDN_DV]
            q = xq * lax.rsqrt(jnp.sum(xq * xq, axis=-1, keepdims=True) + RMS_EPS) * (GDN_DK ** -0.5)
            k = xk * lax.rsqrt(jnp.sum(xk * xk, axis=-1, keepdims=True) + RMS_EPS)
            beta = sm[:, BETA_LO + h:BETA_LO + h + 1]
            gc = gcum[:, G_LO + h:G_LO + h + 1]
            grow = jnp.sum(jnp.where(eye, gc, 0.0), axis=0, keepdims=True)
            decay = jnp.where(tri_incl, jnp.exp(jnp.where(tri_incl, gc - grow, 0.0)), 0.0)
            eg = jnp.exp(gc)
            glast = gc[L - 1:L, :]
            kb = k.astype(BF16)
            qb = q.astype(BF16)
            a = jnp.where(tri_strict, _dot_nt(kb, kb) * decay, 0.0) * beta
            units.append(dict(bi=bi, r0=r0, h=h, a=a, eg=eg, eglast=jnp.exp(glast), qb=qb,
                              rhs=jnp.concatenate([beta * v, (beta * eg) * k], axis=1),
                              qk=(_dot_nt(qb, kb) * decay).astype(BF16),
                              kd=(k * jnp.exp(glast - gc)).astype(BF16)))
    xs = [eye_f - u["a"] for u in units]
    pp = [_split(u["a"]) for u in units]
    m = 2
    while m < L:
        pp = [_split(_dot3(p2, p2)) for p2 in pp]
        xs = [x + _dot3(_split(x), p2) for x, p2 in zip(xs, pp)]
        m *= 2
    for u, x in zip(units, xs):
        uw = _dot3(_split(x), _split(u["rhs"]))
        u["u"] = uw[:, :GDN_DV]
        u["qw"] = jnp.concatenate([u["qb"], uw[:, GDN_DV:].astype(BF16)], axis=0)

    state = {(bi, h): s_ref[bi, h] for bi in range(nb) for h in range(GDN_HEADS)}
    for u in units:
        bi, h, r0 = u["bi"], u["h"], u["r0"]
        s = state[bi, h]
        r = _dot(u["qw"], s.astype(BF16))
        db = (u["u"] - r[L:, :]).astype(BF16)
        o = u["eg"] * r[:L, :] + _dot(u["qk"], db)
        state[bi, h] = u["eglast"] * s + _dot_tn(u["kd"], db)
        on = o * lax.rsqrt(jnp.mean(o * o, axis=-1, keepdims=True) + RMS_EPS) * nw_ref[...]
        zz = z_ref[bi, r0:r0 + L, h * GDN_DV:(h + 1) * GDN_DV]
        o_ref[bi, r0:r0 + L, h * GDN_DV:(h + 1) * GDN_DV] = (on * (zz * _sigmoid(zz))).astype(BF16)
    for (bi, h), s in state.items():
        s_ref[bi, h] = s


def _gdn(gqkv, gz, small, conv_buf, s0, conv_w, norm_w, rows, nb=1):
    b, t, _ = gqkv.shape
    blk = lambda w: pl.BlockSpec((nb, rows, w), lambda bi, c: (bi, c, 0))
    state = pl.BlockSpec((nb, GDN_HEADS, GDN_DK, GDN_DV), lambda bi, c: (bi, 0, 0, 0))
    return pl.pallas_call(
        functools.partial(_gdn_kernel, rows=rows, nb=nb),
        grid=(b // nb, t // rows),
        in_specs=[blk(GDN_CONV_CH), blk(GDN_VW), blk(SMALL_W),
                  pl.BlockSpec((nb, CONV_W - 1, GDN_CONV_CH), lambda bi, c: (bi, 0, 0)),
                  state,
                  pl.BlockSpec((CONV_W, GDN_CONV_CH), lambda bi, c: (0, 0)),
                  pl.BlockSpec((1, GDN_DV), lambda bi, c: (0, 0))],
        out_specs=[blk(GDN_VW), state],
        out_shape=[jax.ShapeDtypeStruct((b, t, GDN_VW), BF16),
                   jax.ShapeDtypeStruct((b, GDN_HEADS, GDN_DK, GDN_DV), F32)],
        scratch_shapes=[pltpu.VMEM((nb, rows + ROW_GROUP, GDN_CONV_CH), F32)],
        compiler_params=_cparams(("parallel", "arbitrary")),
        name="gdn",
    )(gqkv, gz, small, conv_buf, s0, conv_w, norm_w)


def _post_kernel(fo_ref, go_ref, ma_ref, mb_ref, x_ref, wfo_ref, wgo_ref, wout_ref, gb_ref, nw_ref, rw_ref, rb_ref,
                 tri_ref, cin_ref, x2_ref, h2_ref, gates_ref, ir_ref, cnt_ref, carry):
    i = pl.program_id(0)

    @pl.when(i == 0)
    def _():
        carry[...] = cin_ref[...]

    ya = _dot(fo_ref[...], wfo_ref[...])
    yb = _dot(go_ref[...], wgo_ref[...])
    merged = _sigmoid(ma_ref[...] + gb_ref[0:1, :]) * ya + _sigmoid(mb_ref[...] + gb_ref[1:2, :]) * yb
    x2 = x_ref[...] + _dot(merged.astype(BF16), wout_ref[...])
    x2_ref[...] = x2
    h2 = (x2 * lax.rsqrt(jnp.mean(x2 * x2, axis=-1, keepdims=True) + RMS_EPS)) * nw_ref[...]
    h2_ref[...] = h2
    logits = _dot(h2.astype(BF16), rw_ref[...]) + rb_ref[...]
    lane = lax.broadcasted_iota(jnp.int32, logits.shape, 1).astype(F32)
    work = logits
    vals, hits = [], []
    for _ in range(TOP_K):
        m = jnp.max(work, axis=-1, keepdims=True)
        idx = jnp.min(jnp.where(work == m, lane, float(LANES)), axis=-1, keepdims=True)
        hit = lane == idx
        vals.append(m)
        hits.append((hit, idx))
        work = jnp.where(hit, -jnp.inf, work)
    es = [jnp.exp(v - vals[0]) for v in vals]
    denom = es[0] + es[1] + es[2] + es[3]
    cnt = jnp.zeros(logits.shape, F32)
    for hit, _ in hits:
        cnt = cnt + hit.astype(F32)
    base = _dot(tri_ref[...], cnt.astype(BF16)) + carry[...]
    gates = jnp.zeros(logits.shape, F32)
    ir = jnp.zeros(logits.shape, F32)
    for kk, (hit, idx) in enumerate(hits):
        rank = jnp.sum(jnp.where(hit, base, 0.0), axis=-1, keepdims=True)
        gates = gates + jnp.where(lane == float(kk), es[kk] / denom, 0.0)
        ir = ir + jnp.where(lane == float(kk), idx, 0.0) + jnp.where(lane == float(TOP_K + kk), rank, 0.0)
    gates_ref[...] = gates
    ir_ref[...] = jnp.transpose(ir)[0:2 * TOP_K, :].astype(jnp.int32)
    carry[...] = carry[...] + jnp.sum(cnt, axis=0, keepdims=True)
    cnt_ref[...] = carry[...]


def _post(fo, go, ma, mb, x, wfo, wgo, wout, gate_bias, nw, rw, rb, cnt_in, tm):
    n = x.shape[0]
    row = lambda w: pl.BlockSpec((tm, w), lambda i: (i, 0))
    const = lambda shape: pl.BlockSpec(shape, lambda i: (0, 0))
    tri = jnp.asarray(np.tril(np.ones((tm, tm), np.float32), -1), BF16)
    return pl.pallas_call(
        _post_kernel,
        grid=(n // tm,),
        in_specs=[row(FOX_W), row(GDN_VW), row(D_MODEL), row(D_MODEL), row(D_MODEL),
                  const((FOX_W, D_MODEL)), const((GDN_VW, D_MODEL)), const((D_MODEL, D_MODEL)),
                  const((2, D_MODEL)), const((1, D_MODEL)),
                  const((D_MODEL, LANES)), const((1, LANES)),
                  const((tm, tm)), const((1, LANES))],
        out_specs=[row(D_MODEL), row(D_MODEL), row(LANES), pl.BlockSpec((2 * TOP_K, tm), lambda i: (0, i)),
                   const((1, LANES))],
        out_shape=[jax.ShapeDtypeStruct((n, D_MODEL), F32), jax.ShapeDtypeStruct((n, D_MODEL), F32),
                   jax.ShapeDtypeStruct((n, LANES), F32), jax.ShapeDtypeStruct((2 * TOP_K, n), jnp.int32),
                   jax.ShapeDtypeStruct((1, LANES), F32)],
        scratch_shapes=[pltpu.VMEM((1, LANES), F32)],
        compiler_params=_cparams(("arbitrary",)),
        name="post_router",
    )(fo, go, ma, mb, x, wfo, wgo, wout, gate_bias, nw, rw, rb, tri, cnt_in)


def _dispatch_kernel(pad_ref, dest_ref, hp_ref, hs_ref, xs_ref, sem, zbuf, zsem, *, tm, np_tiles):
    i = pl.program_id(0)
    zrows = zbuf.shape[0]

    @pl.when(i == 0)
    def _():
        zbuf[...] = jnp.zeros(zbuf.shape, zbuf.dtype)
        for phase in ("start", "wait"):
            def run(cp, phase=phase):
                cp.start() if phase == "start" else cp.wait()

            def per_expert(e, carry, run=run):
                first = pad_ref[e]
                pos = pad_ref[2 * N_EXPERTS + e]
                groups = pad_ref[3 * N_EXPERTS + e]

                def one(j, c2):
                    run(pltpu.make_async_copy(zbuf.at[pl.ds(0, 1), :], xs_ref.at[pl.ds(first + j, 1), :], zsem))
                    return c2

                lax.fori_loop(0, pad_ref[N_EXPERTS + e], one, 0)
                for b in reversed(range(PAD_GROUP_BITS)):
                    size = ROW_GROUP << b
                    bit = (groups >> b) & 1

                    @pl.when(bit == 1)
                    def _(pos=pos, size=size):
                        run(pltpu.make_async_copy(zbuf.at[pl.ds(0, size), :],
                                                  xs_ref.at[pl.ds(pl.multiple_of(pos, ROW_GROUP), size), :], zsem))

                    pos = pos + bit * size
                return carry

            lax.fori_loop(0, N_EXPERTS, per_expert, 0)

            def tail(j, carry, run=run):
                row = pl.multiple_of(pad_ref[4 * N_EXPERTS] + j * zrows, zrows)
                run(pltpu.make_async_copy(zbuf, xs_ref.at[pl.ds(row, zrows), :], zsem))
                return carry

            lax.fori_loop(0, pad_ref[4 * N_EXPERTS + 1], tail, 0)

    group = min(DMA_ROWS_PER_TRIP, tm)

    def scatter(h_ref):
        def issue(tg, carry):
            t0 = pl.multiple_of(tg * group, group)
            for r in range(group):
                for kk in range(TOP_K):
                    d = dest_ref[(t0 + r) * TOP_K + kk]
                    pltpu.make_async_copy(h_ref.at[pl.ds(t0 + r, 1), :], xs_ref.at[pl.ds(d, 1), :],
                                          sem).start(priority=kk % 2)
            return carry

        lax.fori_loop(0, tm // group, issue, 0)
        for kk in range(TOP_K):
            pltpu.make_async_copy(h_ref, xs_ref.at[pl.ds(0, tm), :], sem).wait()

    @pl.when(i < np_tiles)
    def _():
        scatter(hp_ref)

    @pl.when(i >= np_tiles)
    def _():
        scatter(hs_ref)


def _dispatch(dest_flat, h_p, h_s, pad_tab, rows, tm):
    np_tiles, ns_tiles = h_p.shape[0] // tm, h_s.shape[0] // tm
    return pl.pallas_call(
        functools.partial(_dispatch_kernel, tm=tm, np_tiles=np_tiles),
        grid_spec=pltpu.PrefetchScalarGridSpec(
            num_scalar_prefetch=1, grid=(np_tiles + ns_tiles,),
            in_specs=[pl.BlockSpec((tm * TOP_K,), lambda i, pad: (i,), memory_space=pltpu.SMEM),
                      pl.BlockSpec((tm, D_MODEL), lambda i, pad: (jnp.minimum(i, np_tiles - 1), 0)),
                      pl.BlockSpec((tm, D_MODEL), lambda i, pad: (jnp.maximum(i - np_tiles, 0), 0))],
            out_specs=pl.BlockSpec(memory_space=pl.ANY),
            scratch_shapes=[pltpu.SemaphoreType.DMA(()), pltpu.VMEM((ZERO_ROWS, D_MODEL), F32),
                            pltpu.SemaphoreType.DMA(())]),
        out_shape=jax.ShapeDtypeStruct((rows, D_MODEL), F32),
        compiler_params=_cparams(("arbitrary",)),
        name="moe_dispatch",
    )(pad_tab, dest_flat, h_p, h_s)


def _expert_kernel(be_ref, nu_ref, x_ref, wgu_ref, bgu_ref, wd_ref, bd_ref, y_ref, wgu_b, wd_b):
    i = pl.program_id(0)
    changed = jnp.logical_or(i == 0, be_ref[i] != be_ref[jnp.maximum(i - 1, 0)])

    @pl.when(changed)
    def _():
        for f in range(0, 2 * D_FF, FF_CHUNK):
            wgu_b[:, f:f + FF_CHUNK] = wgu_ref[:, f:f + FF_CHUNK].astype(BF16)
        for f in range(0, D_FF, FF_CHUNK):
            wd_b[f:f + FF_CHUNK, :] = wd_ref[f:f + FF_CHUNK, :].astype(BF16)

    @pl.when(i < nu_ref[0])
    def _():
        x = x_ref[...].astype(BF16)
        acc = jnp.zeros(y_ref.shape, F32)
        for f in range(0, D_FF, FF_CHUNK):
            gate = _dot(x, wgu_b[:, f:f + FF_CHUNK]) + bgu_ref[:, f:f + FF_CHUNK]
            up = _dot(x, wgu_b[:, D_FF + f:D_FF + f + FF_CHUNK]) + bgu_ref[:, D_FF + f:D_FF + f + FF_CHUNK]
            gate = jnp.minimum(gate, SWIGLU_LIMIT)
            up = jnp.clip(up, -SWIGLU_LIMIT, SWIGLU_LIMIT)
            act = (up + 1.0) * (gate * _sigmoid(SWIGLU_ALPHA * gate))
            acc = acc + _dot(act.astype(BF16), wd_b[f:f + FF_CHUNK, :])
        y_ref[...] = acc + bd_ref[...]

    @pl.when(i >= nu_ref[0])
    def _():
        y_ref[...] = jnp.zeros(y_ref.shape, F32)


def _experts(block_e, n_used, xs, w_gu, b_gu, w_down, b_down):
    rows = xs.shape[0]
    tb = EXPERT_BLOCK
    grid_spec = pltpu.PrefetchScalarGridSpec(
        num_scalar_prefetch=2,
        grid=(rows // tb,),
        in_specs=[pl.BlockSpec((tb, D_MODEL), lambda i, be, nu: (jnp.minimum(i, jnp.maximum(nu[0] - 1, 0)), 0)),
                  pl.BlockSpec((None, D_MODEL, 2 * D_FF), lambda i, be, nu: (be[i], 0, 0)),
                  pl.BlockSpec((None, 1, 2 * D_FF), lambda i, be, nu: (be[i], 0, 0)),
                  pl.BlockSpec((None, D_FF, D_MODEL), lambda i, be, nu: (be[i], 0, 0)),
                  pl.BlockSpec((None, 1, D_MODEL), lambda i, be, nu: (be[i], 0, 0))],
        out_specs=pl.BlockSpec((tb, D_MODEL), lambda i, be, nu: (i, 0)),
        scratch_shapes=[pltpu.VMEM((D_MODEL, 2 * D_FF), BF16), pltpu.VMEM((D_FF, D_MODEL), BF16)],
    )
    return pl.pallas_call(
        _expert_kernel,
        grid_spec=grid_spec,
        out_shape=jax.ShapeDtypeStruct((rows, D_MODEL), F32),
        compiler_params=_cparams(("arbitrary",)),
        name="moe_experts",
    )(block_e, n_used, xs, w_gu, b_gu.reshape(N_EXPERTS, 1, 2 * D_FF), w_down, b_down.reshape(N_EXPERTS, 1, D_MODEL))


def _combine_kernel(dest_ref, dnext_ref, x2_ref, gates_ref, fw_ref, ys_ref, y_ref, buf, sems, *, tm, nt):
    i = pl.program_id(0)
    slot = lax.rem(i, 2)
    group = min(DMA_ROWS_PER_TRIP, tm)

    def gather(d_ref, s):
        def issue(tg, carry):
            t0 = pl.multiple_of(tg * group, group)
            for r in range(group):
                for kk in range(TOP_K):
                    d = d_ref[(t0 + r) * TOP_K + kk]
                    pltpu.make_async_copy(ys_ref.at[pl.ds(d, 1), :], buf.at[s, kk, pl.ds(t0 + r, 1), :],
                                          sems.at[s]).start(priority=kk % 2)
            return carry

        lax.fori_loop(0, tm // group, issue, 0)

    @pl.when(i == 0)
    def _():
        gather(dest_ref, 0)

    @pl.when(i + 1 < nt)
    def _():
        gather(dnext_ref, 1 - slot)

    for kk in range(TOP_K):
        pltpu.make_async_copy(ys_ref.at[pl.ds(0, tm), :], buf.at[slot, kk], sems.at[slot]).wait()
    gates = gates_ref[...]
    out = x2_ref[...]
    for kk in range(TOP_K):
        out = out + gates[:, kk:kk + 1] * buf[slot, kk]
    y_ref[...] = (out * lax.rsqrt(jnp.mean(out * out, axis=-1, keepdims=True) + RMS_EPS)) * fw_ref[...]


def _combine(dest_flat, x2, gates, final_w, ys, tm):
    n = x2.shape[0]
    nt = n // tm
    return pl.pallas_call(
        functools.partial(_combine_kernel, tm=tm, nt=nt),
        grid=(nt,),
        in_specs=[pl.BlockSpec((tm * TOP_K,), lambda i: (i,), memory_space=pltpu.SMEM),
                  pl.BlockSpec((tm * TOP_K,), lambda i: (jnp.minimum(i + 1, nt - 1),), memory_space=pltpu.SMEM),
                  pl.BlockSpec((tm, D_MODEL), lambda i: (i, 0)),
                  pl.BlockSpec((tm, LANES), lambda i: (i, 0)),
                  pl.BlockSpec((1, D_MODEL), lambda i: (0, 0)),
                  pl.BlockSpec(memory_space=pl.ANY)],
        out_specs=pl.BlockSpec((tm, D_MODEL), lambda i: (i, 0)),
        out_shape=jax.ShapeDtypeStruct((n, D_MODEL), F32),
        scratch_shapes=[pltpu.VMEM((2, TOP_K, tm, D_MODEL), F32), pltpu.SemaphoreType.DMA((2,))],
        compiler_params=_cparams(("arbitrary",)),
        name="moe_combine",
    )(dest_flat, dest_flat, x2, gates, final_w, ys)


def _row_tile(n, want):
    t = min(want, n)
    while n % t:
        t //= 2
    return t


def kernel(x_prompt, x_sample, cache_fox_k, cache_fox_v, cache_fox_logf, state_gdn, state_gdn_conv, attn_norm_w, w_in, fox_f_bias, gdn_conv_w, gdn_a_log, gdn_dt_bias, gdn_norm_w, gate_bias, fox_w_o, gdn_w_o, w_out, ffn_norm_w, router_w, router_b, expert_w_gu, expert_b_gu, expert_w_down, expert_b_down, final_norm_w):
    l = 0
    bp, sp, d = x_prompt.shape
    bs, ts, _ = x_sample.shape
    past = cache_fox_k.shape[2]
    n_p, n_s = bp * sp, bs * ts

    w = w_in[l]
    o_ff = 3 * FOX_W
    o_gqkv = o_ff + FOX_HEADS
    o_gz = o_gqkv + GDN_CONV_CH
    o_ga = o_gz + GDN_VW
    o_gb = o_ga + GDN_HEADS
    o_ma = o_gb + GDN_HEADS
    w_r = jnp.concatenate(
        [w[:, :o_ff], w[:, o_gqkv:o_gz], w[:, o_gz:o_ga], w[:, o_ma:], w[:, o_ff:o_gqkv], w[:, o_ga:o_ma],
         jnp.zeros((d, SMALL_W - FOX_HEADS - 2 * GDN_HEADS), w.dtype)], axis=1).astype(BF16)
    sb = jnp.zeros((8, SMALL_W), F32)
    sb = sb.at[0, LOGF_LO:LOGF_LO + FOX_HEADS].set(fox_f_bias[l])
    sb = sb.at[0, G_LO:G_LO + GDN_HEADS].set(gdn_dt_bias[l])
    sb = sb.at[1, G_LO:G_LO + GDN_HEADS].set(gdn_a_log[l])
    anw = attn_norm_w[l].reshape(1, d)
    wfo = fox_w_o[l].astype(BF16)
    wgo = gdn_w_o[l].astype(BF16)
    wout = w_out[l].astype(BF16)
    fnw = ffn_norm_w[l].reshape(1, d)
    rw = jnp.concatenate([router_w[l], jnp.zeros((d, LANES - N_EXPERTS), F32)], axis=1).astype(BF16)
    rb = jnp.concatenate([router_b[l], jnp.full((LANES - N_EXPERTS,), -jnp.inf, F32)]).reshape(1, LANES)
    gnw = gdn_norm_w[l].reshape(1, GDN_DV)
    final_w = final_norm_w.reshape(1, d)

    def mixer(x2d, b, t, aug):
        tm = _row_tile(t if aug else x2d.shape[0], 256)
        q, k, v, kb, vb, gqkv, gz, ma, mb, small = _proj(x2d, anw, w_r, sb, tm, t if aug else None)
        return dict(q=q.reshape(b, t, -1), k=k, v=v, kb=kb.reshape(b, t, -1), vb=vb.reshape(b, t, -1),
                    gqkv=gqkv.reshape(b, t, GDN_CONV_CH), gz=gz.reshape(b, t, GDN_VW), ma=ma, mb=mb,
                    small=small.reshape(b, t, SMALL_W))

    pp = mixer(x_prompt.reshape(n_p, d), bp, sp, True)
    logf_p = pp["small"][:, :, LOGF_LO:LOGF_LO + FOX_HEADS]
    fo_p = _attn_prompt(pp["q"], pp["kb"], pp["vb"], _row_tile(sp, ATTN_TQ), _row_tile(sp, ATTN_TK))
    rows_p = _row_tile(sp, 4 * CHUNK)
    go_p, state_p = _gdn(pp["gqkv"], pp["gz"], pp["small"], jnp.zeros((bp, CONV_W - 1, GDN_CONV_CH), F32),
                         jnp.zeros((bp, GDN_HEADS, GDN_DK, GDN_DV), F32), gdn_conv_w[l], gnw, rows_p)
    conv_p = pp["gqkv"][:, sp - (CONV_W - 1):, :]

    ps = mixer(x_sample.reshape(n_s, d), bs, ts, False)
    logf_s = ps["small"][:, :, LOGF_LO:LOGF_LO + FOX_HEADS]
    tot = past + ts
    tot_pad = -(-tot // LANES) * LANES
    logf_all = jnp.concatenate([cache_fox_logf[l].astype(F32), logf_s,
                                jnp.zeros((bs, tot_pad - tot, FOX_HEADS), F32)], axis=1)
    c_s = _seq_cumsum(jnp.transpose(logf_all, (0, 2, 1)))
    fo_s = _attn_step(ps["q"], ps["kb"], ps["vb"], jnp.transpose(cache_fox_k[l], (0, 2, 3, 1)),
                      jnp.transpose(cache_fox_v[l], (0, 2, 3, 1)), c_s)
    t_pad = -(-ts // CHUNK) * CHUNK
    padt = lambda a: jnp.pad(a, ((0, 0), (0, t_pad - ts), (0, 0)))
    go_s, state_s = _gdn(padt(ps["gqkv"]), padt(ps["gz"]), padt(ps["small"]), state_gdn_conv[l], state_gdn[l],
                         gdn_conv_w[l], gnw, CHUNK, nb=_row_tile(bs, GDN_SEQS_PER_STEP))
    go_s = go_s[:, :ts, :]
    conv_s = ps["gqkv"][:, ts - (CONV_W - 1):, :]

    tm_p = _row_tile(n_p, 512)
    tm_s = _row_tile(n_s, 512)
    x2_p, h2_p, gates_p, ir_p, cnt_p = _post(fo_p.reshape(n_p, FOX_W), go_p.reshape(n_p, GDN_VW), pp["ma"], pp["mb"],
                                             x_prompt.reshape(n_p, d), wfo, wgo, wout, gate_bias[l], fnw, rw, rb,
                                             jnp.zeros((1, LANES), F32), tm_p)
    x2_s, h2_s, gates_s, ir_s, cnt_s = _post(fo_s.reshape(n_s, FOX_W), go_s.reshape(n_s, GDN_VW), ps["ma"], ps["mb"],
                                             x_sample.reshape(n_s, d), wfo, wgo, wout, gate_bias[l], fnw, rw, rb,
                                             cnt_p, tm_s)

    tb = EXPERT_BLOCK
    nk = (n_p + n_s) * TOP_K
    n_blocks = -(-nk // tb) + N_EXPERTS
    counts = cnt_s[0, :N_EXPERTS].astype(jnp.int32)
    padded = (counts + tb - 1) // tb * tb
    pad_end = jnp.cumsum(padded)
    pad_start = pad_end - padded
    block_pos = jnp.arange(n_blocks, dtype=jnp.int32) * tb
    block_e = jnp.minimum(jnp.sum((pad_end[None, :] <= block_pos[:, None]).astype(jnp.int32), axis=1), N_EXPERTS - 1)
    n_used = (pad_end[-1:] // tb).astype(jnp.int32)
    experts = jnp.arange(N_EXPERTS, dtype=jnp.int32)[:, None, None]

    def dest(ir):
        start = jnp.sum(jnp.where(ir[None, :TOP_K] == experts, pad_start[:, None, None], 0), axis=0)
        return jnp.transpose(start + ir[TOP_K:]).reshape(-1)
    dest_p, dest_s = dest(ir_p), dest(ir_s)

    pad_first = pad_start + counts
    pad_aligned = (pad_first + ROW_GROUP - 1) // ROW_GROUP * ROW_GROUP
    rows = n_blocks * tb
    pad_tab = jnp.concatenate([pad_first, pad_aligned - pad_first, pad_aligned, (pad_end - pad_aligned) // ROW_GROUP,
                               pad_end[-1:], (rows - pad_end[-1:]) // ZERO_ROWS]).astype(jnp.int32)
    tm_d = _row_tile(n_s, 512)
    assert n_p % tm_d == 0 and tb % ZERO_ROWS == 0
    xs = _dispatch(jnp.concatenate([dest_p, dest_s]), h2_p, h2_s, pad_tab, rows, tm_d)
    ys = _experts(block_e, n_used, xs, expert_w_gu[l], expert_b_gu[l], expert_w_down[l], expert_b_down[l])
    y_p = _combine(dest_p, x2_p, gates_p, final_w, ys, tm_p)
    y_s = _combine(dest_s, x2_s, gates_s, final_w, ys, tm_s)

    hd = (FOX_HEADS, FOX_HEAD_DIM)
    seq_minor = lambda a: jnp.transpose(a.reshape(1, bp, *hd, sp), (0, 1, 4, 2, 3))
    return (y_p.reshape(bp, sp, d), y_s.reshape(bs, ts, d),
            seq_minor(pp["k"]), seq_minor(pp["v"]), logf_p[None],
            state_p[None], conv_p[None],
            ps["k"].reshape(1, bs, ts, *hd), ps["v"].reshape(1, bs, ts, *hd), logf_s[None],
            state_s[None], conv_s[None])
```

```python
import functools

import numpy as np
import jax
import jax.numpy as jnp
from jax import lax
from jax.experimental import pallas as pl
from jax.experimental.pallas import tpu as pltpu

F32 = jnp.float32
BF16 = jnp.bfloat16
HIGHEST = lax.Precision.HIGHEST

D_MODEL = 1024
FOX_HEADS = 8
FOX_HEAD_DIM = 64
FOX_W = FOX_HEADS * FOX_HEAD_DIM
FOX_SCALE = FOX_HEAD_DIM ** -0.5
GDN_HEADS = 4
GDN_DK = 128
GDN_DV = 128
GDN_KW = GDN_HEADS * GDN_DK
GDN_VW = GDN_HEADS * GDN_DV
GDN_CONV_CH = 2 * GDN_KW + GDN_VW
CONV_W = 4
CHUNK = 64
N_EXPERTS = 32
TOP_K = 4
D_FF = D_MODEL
SWIGLU_LIMIT = 7.0
SWIGLU_ALPHA = 1.702
RMS_EPS = 1e-6
NEG_INF = -1e30

LANES = 128
SMALL_W = LANES
LOGF_LO, G_LO, BETA_LO = 0, FOX_HEADS, FOX_HEADS + GDN_HEADS
C_Q, C_K, C_V = 0, FOX_W, 2 * FOX_W
C_GQKV = 3 * FOX_W
C_GZ = C_GQKV + GDN_CONV_CH
C_MA = C_GZ + GDN_VW
C_MB = C_MA + D_MODEL
C_SMALL = C_MB + D_MODEL
W_R = C_SMALL + SMALL_W

VMEM_LIMIT = 56 * 1024 * 1024
ATTN_TQ = 512
ATTN_TK = 512
ATTN_HEADS_PER_LOOP = 4
ATTN_DIAG_KEYS = 256
GDN_SEQS_PER_STEP = 4
ROW_GROUP = 8
DMA_ROWS_PER_TRIP = 8
EXPERT_BLOCK = 512
PAD_GROUP_BITS = (EXPERT_BLOCK // ROW_GROUP).bit_length() - 1
ZERO_ROWS = ROW_GROUP << (PAD_GROUP_BITS - 1)
FF_CHUNK = 512


def _cparams(sem, vmem=VMEM_LIMIT):
    return pltpu.CompilerParams(dimension_semantics=sem, vmem_limit_bytes=vmem)


def _softplus(x):
    return jnp.maximum(x, 0.0) + jnp.log1p(jnp.exp(-jnp.abs(x)))


def _sigmoid(x):
    return jax.nn.sigmoid(x)


def _dot(a, b, precision=None):
    return jnp.dot(a, b, preferred_element_type=F32, precision=precision)


def _dot_nt(a, b, precision=None):
    return lax.dot_general(a, b, (((1,), (1,)), ((), ())), preferred_element_type=F32, precision=precision)


def _dot_tn(a, b, precision=None):
    return lax.dot_general(a, b, (((0,), (0,)), ((), ())), preferred_element_type=F32, precision=precision)


def _head_tiles(x):
    tiles = []
    for pair in range(FOX_HEADS // 2):
        t = x[:, pair * LANES:(pair + 1) * LANES]
        tiles += [t, pltpu.roll(t, FOX_HEAD_DIM, axis=1)]
    return tiles


def _proj_kernel(x_ref, nw_ref, w_ref, sb_ref, *rest, aug, tiles_per_seq):
    if aug:
        tri_ref, place_ref, q_ref, k_ref, v_ref, kb_ref, vb_ref, gqkv_ref, gz_ref, ma_ref, mb_ref, small_ref, carry = rest
    else:
        q_ref, k_ref, v_ref, kb_ref, vb_ref, gqkv_ref, gz_ref, ma_ref, mb_ref, small_ref = rest
    x = x_ref[...]
    ms = jnp.mean(x * x, axis=-1, keepdims=True)
    h = ((x * lax.rsqrt(ms + RMS_EPS)) * nw_ref[...]).astype(BF16)

    def sec(lo, width):
        return _dot(h, w_ref[:, lo:lo + width])

    q = sec(C_Q, FOX_W) * FOX_SCALE
    k = sec(C_K, FOX_W)
    v = sec(C_V, FOX_W)
    if aug:
        k_ref[...] = jnp.transpose(k)
        v_ref[...] = jnp.transpose(v)
    else:
        k_ref[...] = k
        v_ref[...] = v
    gqkv_ref[...] = sec(C_GQKV, GDN_CONV_CH)
    gz_ref[...] = sec(C_GZ, GDN_VW)
    ma_ref[...] = sec(C_MA, D_MODEL)
    mb_ref[...] = sec(C_MB, D_MODEL)
    z = sec(C_SMALL, SMALL_W) + sb_ref[0:1, :]
    lane = lax.broadcasted_iota(jnp.int32, z.shape, 1)
    logf = -_softplus(-z)
    g = -jnp.exp(sb_ref[1:2, :]) * _softplus(z)
    beta = _sigmoid(z)
    small_ref[...] = jnp.where(lane < G_LO, logf, jnp.where(lane < BETA_LO, g, beta))
    if not aug:
        q_ref[...] = q.astype(BF16)
        kb_ref[...] = k.astype(BF16)
        vb_ref[...] = v.astype(BF16)
        return

    @pl.when(pl.program_id(0) % tiles_per_seq == 0)
    def _():
        carry[...] = jnp.zeros(carry.shape, F32)

    lf = jnp.where(lane < G_LO, logf, 0.0)
    l1 = lf.astype(BF16)
    r1 = lf - l1.astype(F32)
    l2 = r1.astype(BF16)
    l3 = (r1 - l2.astype(F32)).astype(BF16)
    tri = tri_ref[...]
    c = (_dot(tri, l1) + (_dot(tri, l2) + _dot(tri, l3))) + carry[...]
    carry[...] = c[c.shape[0] - 1:, :]
    nc = -c
    p1 = nc.astype(BF16).astype(F32)
    r1 = nc - p1
    p2 = r1.astype(BF16).astype(F32)
    p3 = (r1 - p2).astype(BF16).astype(F32)
    parts = p1 + pltpu.roll(p2, FOX_HEADS, axis=1) + pltpu.roll(p3, 2 * FOX_HEADS, axis=1)
    ext = _dot(parts.astype(BF16), place_ref[...])
    low = lane < FOX_HEAD_DIM
    ones3 = jnp.where((lane >= FOX_HEAD_DIM) & (lane < FOX_HEAD_DIM + 3), 1.0, 0.0)
    one1 = jnp.where(lane == FOX_HEAD_DIM, 1.0, 0.0)
    for hd, (qt, kt, vt) in enumerate(zip(_head_tiles(q), _head_tiles(k), _head_tiles(v))):
        cols = slice(hd * LANES, (hd + 1) * LANES)
        q_ref[:, cols] = jnp.where(low, qt, ones3).astype(BF16)
        kb_ref[:, cols] = jnp.where(low, kt, ext[:, cols]).astype(BF16)
        vb_ref[:, cols] = jnp.where(low, vt, one1).astype(BF16)


def _proj(x2d, nw, w_r, sb, tm, seq=None):
    n = x2d.shape[0]
    aug = seq is not None
    row = lambda w: pl.BlockSpec((tm, w), lambda i: (i, 0))
    const = lambda shape: pl.BlockSpec(shape, lambda i: (0, 0))
    aw = FOX_HEADS * LANES if aug else FOX_W
    outs = [(aw, BF16), (FOX_W, F32), (FOX_W, F32), (aw, BF16), (aw, BF16), (GDN_CONV_CH, F32),
            (GDN_VW, F32), (D_MODEL, F32), (D_MODEL, F32), (SMALL_W, F32)]
    in_specs = [row(D_MODEL), const((1, D_MODEL)), const((D_MODEL, W_R)), const((8, SMALL_W))]
    args = [x2d, nw, w_r, sb]
    scratch = []
    tps = seq // tm if aug else 1
    head_spec = pl.BlockSpec((None, FOX_W, tm), lambda i: (i // tps, 0, i % tps))
    head_shape = jax.ShapeDtypeStruct((n // seq if aug else 1, FOX_W, seq if aug else tm), F32)
    if aug:
        place = np.zeros((LANES, FOX_HEADS * LANES), np.float32)
        for hd in range(FOX_HEADS):
            for j in range(3):
                place[j * FOX_HEADS + hd, hd * LANES + FOX_HEAD_DIM + j] = 1.0
        in_specs += [const((tm, tm)), const((LANES, FOX_HEADS * LANES))]
        args += [jnp.asarray(np.tril(np.ones((tm, tm), np.float32)), BF16), jnp.asarray(place, BF16)]
        scratch = [pltpu.VMEM((1, SMALL_W), F32)]
    return pl.pallas_call(
        functools.partial(_proj_kernel, aug=aug, tiles_per_seq=(seq // tm if aug else 1)),
        grid=(n // tm,),
        in_specs=in_specs,
        out_specs=[head_spec if aug and idx in (1, 2) else row(w) for idx, (w, _) in enumerate(outs)],
        out_shape=[head_shape if aug and idx in (1, 2) else jax.ShapeDtypeStruct((n, w), dt)
                   for idx, (w, dt) in enumerate(outs)],
        scratch_shapes=scratch,
        compiler_params=_cparams(("arbitrary",)),
        name="proj",
    )(*args)


def _cumsum_kernel(x_ref, tri_ref, lmat_ref, c_ref):
    x = x_ref[...]
    cb = _dot(x, tri_ref[...], HIGHEST)
    tot = jnp.broadcast_to(cb[:, LANES - 1:LANES], cb.shape)
    c_ref[...] = cb + _dot(lmat_ref[...], tot, HIGHEST)


def _seq_cumsum(logf_bhs):
    b, h, s = logf_bhs.shape
    nb = s // LANES
    r = h * nb
    tri = jnp.asarray(np.triu(np.ones((LANES, LANES), np.float32)))
    rr = np.arange(r)
    lmat = jnp.asarray(((rr[:, None] // nb == rr[None, :] // nb) & (rr[None, :] < rr[:, None])).astype(np.float32))
    out = pl.pallas_call(
        _cumsum_kernel,
        grid=(b,),
        in_specs=[pl.BlockSpec((None, r, LANES), lambda i: (i, 0, 0)),
                  pl.BlockSpec((LANES, LANES), lambda i: (0, 0)),
                  pl.BlockSpec((r, r), lambda i: (0, 0))],
        out_specs=pl.BlockSpec((None, r, LANES), lambda i: (i, 0, 0)),
        out_shape=jax.ShapeDtypeStruct((b, r, LANES), F32),
        compiler_params=_cparams(("parallel",)),
        name="logf_cumsum",
    )(logf_bhs.reshape(b, r, LANES), tri, lmat)
    return out.reshape(b, h, s)


def _attn_prompt_kernel(q_ref, k_ref, v_ref, o_ref, *, tq, tk):
    i = pl.program_id(1)
    q0 = pl.multiple_of(i * tq, tq)
    n_full = (i * tq) // tk
    lane = lax.broadcasted_iota(jnp.int32, (1, LANES), 1)
    sub = ATTN_DIAG_KEYS if tq % ATTN_DIAG_KEYS == 0 else tq
    outs = []
    for g0 in range(0, FOX_HEADS, ATTN_HEADS_PER_LOOP):
        heads = list(range(g0, g0 + ATTN_HEADS_PER_LOOP))

        def step(k0, width, row0, carry, mask, heads=heads):
            rows = tq - row0
            if mask:
                keep = (lax.broadcasted_iota(jnp.int32, (rows, width), 0) + (q0 + row0)
                        >= lax.broadcasted_iota(jnp.int32, (rows, width), 1) + k0)
            ss = [_dot_nt(q_ref[row0:, h * LANES:(h + 1) * LANES], k_ref[pl.ds(k0, width), h * LANES:(h + 1) * LANES])
                  for h in heads]
            out = []
            for (m, acc), s, h in zip(carry, ss, heads):
                if mask:
                    s = jnp.where(keep, s, NEG_INF)
                m_old, acc_old = m[row0:], acc[row0:]
                m_new = jnp.maximum(m_old, jnp.max(s, axis=-1, keepdims=True))
                p = jnp.exp(s - m_new).astype(BF16)
                acc_new = jnp.exp(m_old - m_new) * acc_old + _dot(p, v_ref[pl.ds(k0, width), h * LANES:(h + 1) * LANES])
                if row0:
                    m_new = jnp.concatenate([m[:row0], m_new], axis=0)
                    acc_new = jnp.concatenate([acc[:row0], acc_new], axis=0)
                out.append((m_new, acc_new))
            return tuple(out)

        init = tuple((jnp.full((tq, 1), NEG_INF, F32), jnp.zeros((tq, LANES), F32)) for _ in heads)
        carry = lax.fori_loop(0, n_full, lambda j, cr, step=step: step(pl.multiple_of(j * tk, tk), tk, 0, cr, False),
                              init)
        for jj in range(tq // sub):
            carry = step(pl.multiple_of(q0 + jj * sub, sub), sub, jj * sub, carry, True)
        for _, acc in carry:
            outs.append(acc * (1.0 / acc[:, FOX_HEAD_DIM:FOX_HEAD_DIM + 1]))
    for pair in range(FOX_HEADS // 2):
        o_ref[:, pair * LANES:(pair + 1) * LANES] = jnp.where(
            lane < FOX_HEAD_DIM, outs[2 * pair], pltpu.roll(outs[2 * pair + 1], FOX_HEAD_DIM, axis=1)).astype(BF16)


def _attn_prompt(qa, ka, va, tq, tk):
    b, s, aw = qa.shape
    assert tq % tk == 0, "query blocks must start on a key-tile boundary"
    return pl.pallas_call(
        functools.partial(_attn_prompt_kernel, tq=tq, tk=tk),
        grid=(b, s // tq),
        in_specs=[pl.BlockSpec((None, tq, aw), lambda bi, i: (bi, i, 0)),
                  pl.BlockSpec((None, s, aw), lambda bi, i: (bi, 0, 0)),
                  pl.BlockSpec((None, s, aw), lambda bi, i: (bi, 0, 0))],
        out_specs=pl.BlockSpec((None, tq, FOX_W), lambda bi, i: (bi, i, 0)),
        out_shape=jax.ShapeDtypeStruct((b, s, FOX_W), BF16),
        compiler_params=_cparams(("parallel", "arbitrary")),
        name="attn_prompt",
    )(qa, ka, va)


def _attn_step_kernel(q_ref, kn_ref, vn_ref, kct_ref, vct_ref, c_ref, o_ref, *, past, t):
    rowi = lax.broadcasted_iota(jnp.int32, (t, t), 0)
    coli = lax.broadcasted_iota(jnp.int32, (t, t), 1)
    causal = rowi >= coli
    for h in range(FOX_HEADS):
        cols = slice(h * FOX_HEAD_DIM, (h + 1) * FOX_HEAD_DIM)
        qh = q_ref[:, cols]
        c_c = c_ref[h:h + 1, 0:past]
        c_n = c_ref[h:h + 1, past:past + t]
        c0 = c_n[:, 0:1]
        s_c = _dot(qh, kct_ref[h].astype(BF16)) + (c0 - c_c)
        s_n = jnp.where(causal, _dot_nt(qh, kn_ref[:, cols]) + (c0 - c_n), NEG_INF)
        m = jnp.maximum(jnp.max(s_c, axis=-1, keepdims=True), jnp.max(s_n, axis=-1, keepdims=True))
        p_c = jnp.exp(s_c - m)
        p_n = jnp.exp(s_n - m)
        l = jnp.sum(p_c, axis=-1, keepdims=True) + jnp.sum(p_n, axis=-1, keepdims=True)
        acc = _dot_nt(p_c.astype(BF16), vct_ref[h].astype(BF16)) + _dot(p_n.astype(BF16), vn_ref[:, cols])
        o_ref[:, cols] = (acc * (1.0 / l)).astype(BF16)


def _attn_step(qb, knb, vnb, kc_t, vc_t, c_all):
    b, t, _ = qb.shape
    past = kc_t.shape[-1]
    sp = c_all.shape[-1]
    new = pl.BlockSpec((None, t, FOX_W), lambda bi: (bi, 0, 0))
    old = pl.BlockSpec((None, FOX_HEADS, FOX_HEAD_DIM, past), lambda bi: (bi, 0, 0, 0))
    return pl.pallas_call(
        functools.partial(_attn_step_kernel, past=past, t=t),
        grid=(b,),
        in_specs=[new, new, new, old, old, pl.BlockSpec((None, FOX_HEADS, sp), lambda bi: (bi, 0, 0))],
        out_specs=new,
        out_shape=jax.ShapeDtypeStruct((b, t, FOX_W), BF16),
        compiler_params=_cparams(("parallel",)),
        name="attn_step",
    )(qb, knb, vnb, kc_t, vc_t, c_all)


def _split(a):
    hi = a.astype(BF16)
    return hi, (a - hi.astype(F32)).astype(BF16)


def _dot3(a_parts, b_parts):
    (ah, al), (bh, bl) = a_parts, b_parts
    return _dot(ah, bh) + (_dot(ah, bl) + _dot(al, bh))


def _gdn_kernel(x_ref, z_ref, sm_ref, cb_ref, s0_ref, cw_ref, nw_ref, o_ref, s_ref, xbuf, *, rows, nb):
    c = pl.program_id(1)
    L = CHUNK
    hist = ROW_GROUP - (CONV_W - 1)

    @pl.when(c == 0)
    def _():
        xbuf[:, hist:ROW_GROUP, :] = cb_ref[...]
        s_ref[...] = s0_ref[...]

    xbuf[:, ROW_GROUP:ROW_GROUP + rows, :] = x_ref[...]
    conv = xbuf[:, hist:hist + rows, :] * cw_ref[0:1, :]
    for i in range(1, CONV_W):
        conv = conv + xbuf[:, hist + i:hist + i + rows, :] * cw_ref[i:i + 1, :]
    xbuf[:, hist:ROW_GROUP, :] = xbuf[:, rows + hist:rows + ROW_GROUP, :]
    conv = conv * _sigmoid(conv)

    ri = lax.broadcasted_iota(jnp.int32, (L, L), 0)
    ci = lax.broadcasted_iota(jnp.int32, (L, L), 1)
    tri_incl = ri >= ci
    tri_strict = ri > ci
    eye = ri == ci
    eye_f = eye.astype(F32)
    tril_b = tri_incl.astype(BF16)

    units = []
    for bc in range(nb * (rows // L)):
        bi, ch = divmod(bc, rows // L)
        r0 = ch * L
        sm = sm_ref[bi, r0:r0 + L, :]
        sh, sl = _split(sm)
        sl2 = (sm - sh.astype(F32) - sl.astype(F32)).astype(BF16)
        gcum = _dot(tril_b, sh) + (_dot(tril_b, sl) + _dot(tril_b, sl2))
        for h in range(GDN_HEADS):
            xq = conv[bi, r0:r0 + L, h * GDN_DK:(h + 1) * GDN_DK]
            xk = conv[bi, r0:r0 + L, GDN_KW + h * GDN_DK:GDN_KW + (h + 1) * GDN_DK]
            v = conv[bi, r0:r0 + L, 2 * GDN_KW + h * GDN_DV:2 * GDN_KW + (h + 1) * GDN_DV]
            q = xq * lax.rsqrt(jnp.sum(xq * xq, axis=-1, keepdims=True) + RMS_EPS) * (GDN_DK ** -0.5)
            k = xk * lax.rsqrt(jnp.sum(xk * xk, axis=-1, keepdims=True) + RMS_EPS)
            beta = sm[:, BETA_LO + h:BETA_LO + h + 1]
            gc = gcum[:, G_LO + h:G_LO + h + 1]
            grow = jnp.sum(jnp.where(eye, gc, 0.0), axis=0, keepdims=True)
            decay = jnp.where(tri_incl, jnp.exp(jnp.where(tri_incl, gc - grow, 0.0)), 0.0)
            eg = jnp.exp(gc)
            glast = gc[L - 1:L, :]
            kb = k.astype(BF16)
            qb = q.astype(BF16)
            a = jnp.where(tri_strict, _dot_nt(kb, kb) * decay, 0.0) * beta
            units.append(dict(bi=bi, r0=r0, h=h, a=a, eg=eg, eglast=jnp.exp(glast), qb=qb,
                              rhs=jnp.concatenate([beta * v, (beta * eg) * k], axis=1),
                              qk=(_dot_nt(qb, kb) * decay).astype(BF16),
                              kd=(k * jnp.exp(glast - gc)).astype(BF16)))
    xs = [eye_f - u["a"] for u in units]
    pp = [_split(u["a"]) for u in units]
    m = 2
    while m < L:
        pp = [_split(_dot3(p2, p2)) for p2 in pp]
        xs = [x + _dot3(_split(x), p2) for x, p2 in zip(xs, pp)]
        m *= 2
    for u, x in zip(units, xs):
        uw = _dot3(_split(x), _split(u["rhs"]))
        u["u"] = uw[:, :GDN_DV]
        u["qw"] = jnp.concatenate([u["qb"], uw[:, GDN_DV:].astype(BF16)], axis=0)

    state = {(bi, h): s_ref[bi, h] for bi in range(nb) for h in range(GDN_HEADS)}
    for u in units:
        bi, h, r0 = u["bi"], u["h"], u["r0"]
        s = state[bi, h]
        r = _dot(u["qw"], s.astype(BF16))
        db = (u["u"] - r[L:, :]).astype(BF16)
        o = u["eg"] * r[:L, :] + _dot(u["qk"], db)
        state[bi, h] = u["eglast"] * s + _dot_tn(u["kd"], db)
        on = o * lax.rsqrt(jnp.mean(o * o, axis=-1, keepdims=True) + RMS_EPS) * nw_ref[...]
        zz = z_ref[bi, r0:r0 + L, h * GDN_DV:(h + 1) * GDN_DV]
        o_ref[bi, r0:r0 + L, h * GDN_DV:(h + 1) * GDN_DV] = (on * (zz * _sigmoid(zz))).astype(BF16)
    for (bi, h), s in state.items():
        s_ref[bi, h] = s


def _gdn(gqkv, gz, small, conv_buf, s0, conv_w, norm_w, rows, nb=1):
    b, t, _ = gqkv.shape
    blk = lambda w: pl.BlockSpec((nb, rows, w), lambda bi, c: (bi, c, 0))
    state = pl.BlockSpec((nb, GDN_HEADS, GDN_DK, GDN_DV), lambda bi, c: (bi, 0, 0, 0))
    return pl.pallas_call(
        functools.partial(_gdn_kernel, rows=rows, nb=nb),
        grid=(b // nb, t // rows),
        in_specs=[blk(GDN_CONV_CH), blk(GDN_VW), blk(SMALL_W),
                  pl.BlockSpec((nb, CONV_W - 1, GDN_CONV_CH), lambda bi, c: (bi, 0, 0)),
                  state,
                  pl.BlockSpec((CONV_W, GDN_CONV_CH), lambda bi, c: (0, 0)),
                  pl.BlockSpec((1, GDN_DV), lambda bi, c: (0, 0))],
        out_specs=[blk(GDN_VW), state],
        out_shape=[jax.ShapeDtypeStruct((b, t, GDN_VW), BF16),
                   jax.ShapeDtypeStruct((b, GDN_HEADS, GDN_DK, GDN_DV), F32)],
        scratch_shapes=[pltpu.VMEM((nb, rows + ROW_GROUP, GDN_CONV_CH), F32)],
        compiler_params=_cparams(("parallel", "arbitrary")),
        name="gdn",
    )(gqkv, gz, small, conv_buf, s0, conv_w, norm_w)


def _post_kernel(fo_ref, go_ref, ma_ref, mb_ref, x_ref, wfo_ref, wgo_ref, wout_ref, gb_ref, nw_ref, rw_ref, rb_ref,
                 tri_ref, cin_ref, x2_ref, h2_ref, gates_ref, ir_ref, cnt_ref, carry):
    i = pl.program_id(0)

    @pl.when(i == 0)
    def _():
        carry[...] = cin_ref[...]

    ya = _dot(fo_ref[...], wfo_ref[...])
    yb = _dot(go_ref[...], wgo_ref[...])
    merged = _sigmoid(ma_ref[...] + gb_ref[0:1, :]) * ya + _sigmoid(mb_ref[...] + gb_ref[1:2, :]) * yb
    x2 = x_ref[...] + _dot(merged.astype(BF16), wout_ref[...])
    x2_ref[...] = x2
    h2 = (x2 * lax.rsqrt(jnp.mean(x2 * x2, axis=-1, keepdims=True) + RMS_EPS)) * nw_ref[...]
    h2_ref[...] = h2
    logits = _dot(h2.astype(BF16), rw_ref[...]) + rb_ref[...]
    lane = lax.broadcasted_iota(jnp.int32, logits.shape, 1).astype(F32)
    work = logits
    vals, hits = [], []
    for _ in range(TOP_K):
        m = jnp.max(work, axis=-1, keepdims=True)
        idx = jnp.min(jnp.where(work == m, lane, float(LANES)), axis=-1, keepdims=True)
        hit = lane == idx
        vals.append(m)
        hits.append((hit, idx))
        work = jnp.where(hit, -jnp.inf, work)
    es = [jnp.exp(v - vals[0]) for v in vals]
    denom = es[0] + es[1] + es[2] + es[3]
    cnt = jnp.zeros(logits.shape, F32)
    for hit, _ in hits:
        cnt = cnt + hit.astype(F32)
    base = _dot(tri_ref[...], cnt.astype(BF16)) + carry[...]
    gates = jnp.zeros(logits.shape, F32)
    ir = jnp.zeros(logits.shape, F32)
    for kk, (hit, idx) in enumerate(hits):
        rank = jnp.sum(jnp.where(hit, base, 0.0), axis=-1, keepdims=True)
        gates = gates + jnp.where(lane == float(kk), es[kk] / denom, 0.0)
        ir = ir + jnp.where(lane == float(kk), idx, 0.0) + jnp.where(lane == float(TOP_K + kk), rank, 0.0)
    gates_ref[...] = gates
    ir_ref[...] = jnp.transpose(ir)[0:2 * TOP_K, :].astype(jnp.int32)
    carry[...] = carry[...] + jnp.sum(cnt, axis=0, keepdims=True)
    cnt_ref[...] = carry[...]


def _post(fo, go, ma, mb, x, wfo, wgo, wout, gate_bias, nw, rw, rb, cnt_in, tm):
    n = x.shape[0]
    row = lambda w: pl.BlockSpec((tm, w), lambda i: (i, 0))
    const = lambda shape: pl.BlockSpec(shape, lambda i: (0, 0))
    tri = jnp.asarray(np.tril(np.ones((tm, tm), np.float32), -1), BF16)
    return pl.pallas_call(
        _post_kernel,
        grid=(n // tm,),
        in_specs=[row(FOX_W), row(GDN_VW), row(D_MODEL), row(D_MODEL), row(D_MODEL),
                  const((FOX_W, D_MODEL)), const((GDN_VW, D_MODEL)), const((D_MODEL, D_MODEL)),
                  const((2, D_MODEL)), const((1, D_MODEL)),
                  const((D_MODEL, LANES)), const((1, LANES)),
                  const((tm, tm)), const((1, LANES))],
        out_specs=[row(D_MODEL), row(D_MODEL), row(LANES), pl.BlockSpec((2 * TOP_K, tm), lambda i: (0, i)),
                   const((1, LANES))],
        out_shape=[jax.ShapeDtypeStruct((n, D_MODEL), F32), jax.ShapeDtypeStruct((n, D_MODEL), F32),
                   jax.ShapeDtypeStruct((n, LANES), F32), jax.ShapeDtypeStruct((2 * TOP_K, n), jnp.int32),
                   jax.ShapeDtypeStruct((1, LANES), F32)],
        scratch_shapes=[pltpu.VMEM((1, LANES), F32)],
        compiler_params=_cparams(("arbitrary",)),
        name="post_router",
    )(fo, go, ma, mb, x, wfo, wgo, wout, gate_bias, nw, rw, rb, tri, cnt_in)


def _dispatch_kernel(pad_ref, dest_ref, hp_ref, hs_ref, xs_ref, sem, zbuf, zsem, *, tm, np_tiles):
    i = pl.program_id(0)
    zrows = zbuf.shape[0]

    @pl.when(i == 0)
    def _():
        zbuf[...] = jnp.zeros(zbuf.shape, zbuf.dtype)
        for phase in ("start", "wait"):
            def run(cp, phase=phase):
                cp.start() if phase == "start" else cp.wait()

            def per_expert(e, carry, run=run):
                first = pad_ref[e]
                pos = pad_ref[2 * N_EXPERTS + e]
                groups = pad_ref[3 * N_EXPERTS + e]

                def one(j, c2):
                    run(pltpu.make_async_copy(zbuf.at[pl.ds(0, 1), :], xs_ref.at[pl.ds(first + j, 1), :], zsem))
                    return c2

                lax.fori_loop(0, pad_ref[N_EXPERTS + e], one, 0)
                for b in reversed(range(PAD_GROUP_BITS)):
                    size = ROW_GROUP << b
                    bit = (groups >> b) & 1

                    @pl.when(bit == 1)
                    def _(pos=pos, size=size):
                        run(pltpu.make_async_copy(zbuf.at[pl.ds(0, size), :],
                                                  xs_ref.at[pl.ds(pl.multiple_of(pos, ROW_GROUP), size), :], zsem))

                    pos = pos + bit * size
                return carry

            lax.fori_loop(0, N_EXPERTS, per_expert, 0)

            def tail(j, carry, run=run):
                row = pl.multiple_of(pad_ref[4 * N_EXPERTS] + j * zrows, zrows)
                run(pltpu.make_async_copy(zbuf, xs_ref.at[pl.ds(row, zrows), :], zsem))
                return carry

            lax.fori_loop(0, pad_ref[4 * N_EXPERTS + 1], tail, 0)

    group = min(DMA_ROWS_PER_TRIP, tm)

    def scatter(h_ref):
        def issue(tg, carry):
            t0 = pl.multiple_of(tg * group, group)
            for r in range(group):
                for kk in range(TOP_K):
                    d = dest_ref[(t0 + r) * TOP_K + kk]
                    pltpu.make_async_copy(h_ref.at[pl.ds(t0 + r, 1), :], xs_ref.at[pl.ds(d, 1), :],
                                          sem).start(priority=kk % 2)
            return carry

        lax.fori_loop(0, tm // group, issue, 0)
        for kk in range(TOP_K):
            pltpu.make_async_copy(h_ref, xs_ref.at[pl.ds(0, tm), :], sem).wait()

    @pl.when(i < np_tiles)
    def _():
        scatter(hp_ref)

    @pl.when(i >= np_tiles)
    def _():
        scatter(hs_ref)


def _dispatch(dest_flat, h_p, h_s, pad_tab, rows, tm):
    np_tiles, ns_tiles = h_p.shape[0] // tm, h_s.shape[0] // tm
    return pl.pallas_call(
        functools.partial(_dispatch_kernel, tm=tm, np_tiles=np_tiles),
        grid_spec=pltpu.PrefetchScalarGridSpec(
            num_scalar_prefetch=1, grid=(np_tiles + ns_tiles,),
            in_specs=[pl.BlockSpec((tm * TOP_K,), lambda i, pad: (i,), memory_space=pltpu.SMEM),
                      pl.BlockSpec((tm, D_MODEL), lambda i, pad: (jnp.minimum(i, np_tiles - 1), 0)),
                      pl.BlockSpec((tm, D_MODEL), lambda i, pad: (jnp.maximum(i - np_tiles, 0), 0))],
            out_specs=pl.BlockSpec(memory_space=pl.ANY),
            scratch_shapes=[pltpu.SemaphoreType.DMA(()), pltpu.VMEM((ZERO_ROWS, D_MODEL), F32),
                            pltpu.SemaphoreType.DMA(())]),
        out_shape=jax.ShapeDtypeStruct((rows, D_MODEL), F32),
        compiler_params=_cparams(("arbitrary",)),
        name="moe_dispatch",
    )(pad_tab, dest_flat, h_p, h_s)


def _expert_kernel(be_ref, nu_ref, x_ref, wgu_ref, bgu_ref, wd_ref, bd_ref, y_ref, wgu_b, wd_b):
    i = pl.program_id(0)
    changed = jnp.logical_or(i == 0, be_ref[i] != be_ref[jnp.maximum(i - 1, 0)])

    @pl.when(changed)
    def _():
        for f in range(0, 2 * D_FF, FF_CHUNK):
            wgu_b[:, f:f + FF_CHUNK] = wgu_ref[:, f:f + FF_CHUNK].astype(BF16)
        for f in range(0, D_FF, FF_CHUNK):
            wd_b[f:f + FF_CHUNK, :] = wd_ref[f:f + FF_CHUNK, :].astype(BF16)

    @pl.when(i < nu_ref[0])
    def _():
        x = x_ref[...].astype(BF16)
        acc = jnp.zeros(y_ref.shape, F32)
        for f in range(0, D_FF, FF_CHUNK):
            gate = _dot(x, wgu_b[:, f:f + FF_CHUNK]) + bgu_ref[:, f:f + FF_CHUNK]
            up = _dot(x, wgu_b[:, D_FF + f:D_FF + f + FF_CHUNK]) + bgu_ref[:, D_FF + f:D_FF + f + FF_CHUNK]
            gate = jnp.minimum(gate, SWIGLU_LIMIT)
            up = jnp.clip(up, -SWIGLU_LIMIT, SWIGLU_LIMIT)
            act = (up + 1.0) * (gate * _sigmoid(SWIGLU_ALPHA * gate))
            acc = acc + _dot(act.astype(BF16), wd_b[f:f + FF_CHUNK, :])
        y_ref[...] = acc + bd_ref[...]

    @pl.when(i >= nu_ref[0])
    def _():
        y_ref[...] = jnp.zeros(y_ref.shape, F32)


def _experts(block_e, n_used, xs, w_gu, b_gu, w_down, b_down):
    rows = xs.shape[0]
    tb = EXPERT_BLOCK
    grid_spec = pltpu.PrefetchScalarGridSpec(
        num_scalar_prefetch=2,
        grid=(rows // tb,),
        in_specs=[pl.BlockSpec((tb, D_MODEL), lambda i, be, nu: (jnp.minimum(i, jnp.maximum(nu[0] - 1, 0)), 0)),
                  pl.BlockSpec((None, D_MODEL, 2 * D_FF), lambda i, be, nu: (be[i], 0, 0)),
                  pl.BlockSpec((None, 1, 2 * D_FF), lambda i, be, nu: (be[i], 0, 0)),
                  pl.BlockSpec((None, D_FF, D_MODEL), lambda i, be, nu: (be[i], 0, 0)),
                  pl.BlockSpec((None, 1, D_MODEL), lambda i, be, nu: (be[i], 0, 0))],
        out_specs=pl.BlockSpec((tb, D_MODEL), lambda i, be, nu: (i, 0)),
        scratch_shapes=[pltpu.VMEM((D_MODEL, 2 * D_FF), BF16), pltpu.VMEM((D_FF, D_MODEL), BF16)],
    )
    return pl.pallas_call(
        _expert_kernel,
        grid_spec=grid_spec,
        out_shape=jax.ShapeDtypeStruct((rows, D_MODEL), F32),
        compiler_params=_cparams(("arbitrary",)),
        name="moe_experts",
    )(block_e, n_used, xs, w_gu, b_gu.reshape(N_EXPERTS, 1, 2 * D_FF), w_down, b_down.reshape(N_EXPERTS, 1, D_MODEL))


def _combine_kernel(dest_ref, dnext_ref, x2_ref, gates_ref, fw_ref, ys_ref, y_ref, buf, sems, *, tm, nt):
    i = pl.program_id(0)
    slot = lax.rem(i, 2)
    group = min(DMA_ROWS_PER_TRIP, tm)

    def gather(d_ref, s):
        def issue(tg, carry):
            t0 = pl.multiple_of(tg * group, group)
            for r in range(group):
                for kk in range(TOP_K):
                    d = d_ref[(t0 + r) * TOP_K + kk]
                    pltpu.make_async_copy(ys_ref.at[pl.ds(d, 1), :], buf.at[s, kk, pl.ds(t0 + r, 1), :],
                                          sems.at[s]).start(priority=kk % 2)
            return carry

        lax.fori_loop(0, tm // group, issue, 0)

    @pl.when(i == 0)
    def _():
        gather(dest_ref, 0)

    @pl.when(i + 1 < nt)
    def _():
        gather(dnext_ref, 1 - slot)

    for kk in range(TOP_K):
        pltpu.make_async_copy(ys_ref.at[pl.ds(0, tm), :], buf.at[slot, kk], sems.at[slot]).wait()
    gates = gates_ref[...]
    out = x2_ref[...]
    for kk in range(TOP_K):
        out = out + gates[:, kk:kk + 1] * buf[slot, kk]
    y_ref[...] = (out * lax.rsqrt(jnp.mean(out * out, axis=-1, keepdims=True) + RMS_EPS)) * fw_ref[...]


def _combine(dest_flat, x2, gates, final_w, ys, tm):
    n = x2.shape[0]
    nt = n // tm
    return pl.pallas_call(
        functools.partial(_combine_kernel, tm=tm, nt=nt),
        grid=(nt,),
        in_specs=[pl.BlockSpec((tm * TOP_K,), lambda i: (i,), memory_space=pltpu.SMEM),
                  pl.BlockSpec((tm * TOP_K,), lambda i: (jnp.minimum(i + 1, nt - 1),), memory_space=pltpu.SMEM),
                  pl.BlockSpec((tm, D_MODEL), lambda i: (i, 0)),
                  pl.BlockSpec((tm, LANES), lambda i: (i, 0)),
                  pl.BlockSpec((1, D_MODEL), lambda i: (0, 0)),
                  pl.BlockSpec(memory_space=pl.ANY)],
        out_specs=pl.BlockSpec((tm, D_MODEL), lambda i: (i, 0)),
        out_shape=jax.ShapeDtypeStruct((n, D_MODEL), F32),
        scratch_shapes=[pltpu.VMEM((2, TOP_K, tm, D_MODEL), F32), pltpu.SemaphoreType.DMA((2,))],
        compiler_params=_cparams(("arbitrary",)),
        name="moe_combine",
    )(dest_flat, dest_flat, x2, gates, final_w, ys)


def _row_tile(n, want):
    t = min(want, n)
    while n % t:
        t //= 2
    return t


def kernel(x_prompt, x_sample, cache_fox_k, cache_fox_v, cache_fox_logf, state_gdn, state_gdn_conv, attn_norm_w, w_in, fox_f_bias, gdn_conv_w, gdn_a_log, gdn_dt_bias, gdn_norm_w, gate_bias, fox_w_o, gdn_w_o, w_out, ffn_norm_w, router_w, router_b, expert_w_gu, expert_b_gu, expert_w_down, expert_b_down, final_norm_w):
    l = 0
    bp, sp, d = x_prompt.shape
    bs, ts, _ = x_sample.shape
    past = cache_fox_k.shape[2]
    n_p, n_s = bp * sp, bs * ts

    w = w_in[l]
    o_ff = 3 * FOX_W
    o_gqkv = o_ff + FOX_HEADS
    o_gz = o_gqkv + GDN_CONV_CH
    o_ga = o_gz + GDN_VW
    o_gb = o_ga + GDN_HEADS
    o_ma = o_gb + GDN_HEADS
    w_r = jnp.concatenate(
        [w[:, :o_ff], w[:, o_gqkv:o_gz], w[:, o_gz:o_ga], w[:, o_ma:], w[:, o_ff:o_gqkv], w[:, o_ga:o_ma],
         jnp.zeros((d, SMALL_W - FOX_HEADS - 2 * GDN_HEADS), w.dtype)], axis=1).astype(BF16)
    sb = jnp.zeros((8, SMALL_W), F32)
    sb = sb.at[0, LOGF_LO:LOGF_LO + FOX_HEADS].set(fox_f_bias[l])
    sb = sb.at[0, G_LO:G_LO + GDN_HEADS].set(gdn_dt_bias[l])
    sb = sb.at[1, G_LO:G_LO + GDN_HEADS].set(gdn_a_log[l])
    anw = attn_norm_w[l].reshape(1, d)
    wfo = fox_w_o[l].astype(BF16)
    wgo = gdn_w_o[l].astype(BF16)
    wout = w_out[l].astype(BF16)
    fnw = ffn_norm_w[l].reshape(1, d)
    rw = jnp.concatenate([router_w[l], jnp.zeros((d, LANES - N_EXPERTS), F32)], axis=1).astype(BF16)
    rb = jnp.concatenate([router_b[l], jnp.full((LANES - N_EXPERTS,), -jnp.inf, F32)]).reshape(1, LANES)
    gnw = gdn_norm_w[l].reshape(1, GDN_DV)
    final_w = final_norm_w.reshape(1, d)

    def mixer(x2d, b, t, aug):
        tm = _row_tile(t if aug else x2d.shape[0], 256)
        q, k, v, kb, vb, gqkv, gz, ma, mb, small = _proj(x2d, anw, w_r, sb, tm, t if aug else None)
        return dict(q=q.reshape(b, t, -1), k=k, v=v, kb=kb.reshape(b, t, -1), vb=vb.reshape(b, t, -1),
                    gqkv=gqkv.reshape(b, t, GDN_CONV_CH), gz=gz.reshape(b, t, GDN_VW), ma=ma, mb=mb,
                    small=small.reshape(b, t, SMALL_W))

    pp = mixer(x_prompt.reshape(n_p, d), bp, sp, True)
    logf_p = pp["small"][:, :, LOGF_LO:LOGF_LO + FOX_HEADS]
    fo_p = _attn_prompt(pp["q"], pp["kb"], pp["vb"], _row_tile(sp, ATTN_TQ), _row_tile(sp, ATTN_TK))
    rows_p = _row_tile(sp, 8 * CHUNK)
    go_p, state_p = _gdn(pp["gqkv"], pp["gz"], pp["small"], jnp.zeros((bp, CONV_W - 1, GDN_CONV_CH), F32),
                         jnp.zeros((bp, GDN_HEADS, GDN_DK, GDN_DV), F32), gdn_conv_w[l], gnw, rows_p)
    conv_p = pp["gqkv"][:, sp - (CONV_W - 1):, :]

    ps = mixer(x_sample.reshape(n_s, d), bs, ts, False)
    logf_s = ps["small"][:, :, LOGF_LO:LOGF_LO + FOX_HEADS]
    tot = past + ts
    tot_pad = -(-tot // LANES) * LANES
    logf_all = jnp.concatenate([cache_fox_logf[l].astype(F32), logf_s,
                                jnp.zeros((bs, tot_pad - tot, FOX_HEADS), F32)], axis=1)
    c_s = _seq_cumsum(jnp.transpose(logf_all, (0, 2, 1)))
    fo_s = _attn_step(ps["q"], ps["kb"], ps["vb"], jnp.transpose(cache_fox_k[l], (0, 2, 3, 1)),
                      jnp.transpose(cache_fox_v[l], (0, 2, 3, 1)), c_s)
    t_pad = -(-ts // CHUNK) * CHUNK
    padt = lambda a: jnp.pad(a, ((0, 0), (0, t_pad - ts), (0, 0)))
    go_s, state_s = _gdn(padt(ps["gqkv"]), padt(ps["gz"]), padt(ps["small"]), state_gdn_conv[l], state_gdn[l],
                         gdn_conv_w[l], gnw, CHUNK, nb=_row_tile(bs, GDN_SEQS_PER_STEP))
    go_s = go_s[:, :ts, :]
    conv_s = ps["gqkv"][:, ts - (CONV_W - 1):, :]

    tm_p = _row_tile(n_p, 512)
    tm_s = _row_tile(n_s, 512)
    x2_p, h2_p, gates_p, ir_p, cnt_p = _post(fo_p.reshape(n_p, FOX_W), go_p.reshape(n_p, GDN_VW), pp["ma"], pp["mb"],
                                             x_prompt.reshape(n_p, d), wfo, wgo, wout, gate_bias[l], fnw, rw, rb,
                                             jnp.zeros((1, LANES), F32), tm_p)
    x2_s, h2_s, gates_s, ir_s, cnt_s = _post(fo_s.reshape(n_s, FOX_W), go_s.reshape(n_s, GDN_VW), ps["ma"], ps["mb"],
                                             x_sample.reshape(n_s, d), wfo, wgo, wout, gate_bias[l], fnw, rw, rb,
                                             cnt_p, tm_s)

    tb = EXPERT_BLOCK
    nk = (n_p + n_s) * TOP_K
    n_blocks = -(-nk // tb) + N_EXPERTS
    counts = cnt_s[0, :N_EXPERTS].astype(jnp.int32)
    padded = (counts + tb - 1) // tb * tb
    pad_end = jnp.cumsum(padded)
    pad_start = pad_end - padded
    block_pos = jnp.arange(n_blocks, dtype=jnp.int32) * tb
    block_e = jnp.minimum(jnp.sum((pad_end[None, :] <= block_pos[:, None]).astype(jnp.int32), axis=1), N_EXPERTS - 1)
    n_used = (pad_end[-1:] // tb).astype(jnp.int32)
    experts = jnp.arange(N_EXPERTS, dtype=jnp.int32)[:, None, None]

    def dest(ir):
        start = jnp.sum(jnp.where(ir[None, :TOP_K] == experts, pad_start[:, None, None], 0), axis=0)
        return jnp.transpose(start + ir[TOP_K:]).reshape(-1)
    dest_p, dest_s = dest(ir_p), dest(ir_s)

    pad_first = pad_start + counts
    pad_aligned = (pad_first + ROW_GROUP - 1) // ROW_GROUP * ROW_GROUP
    rows = n_blocks * tb
    pad_tab = jnp.concatenate([pad_first, pad_aligned - pad_first, pad_aligned, (pad_end - pad_aligned) // ROW_GROUP,
                               pad_end[-1:], (rows - pad_end[-1:]) // ZERO_ROWS]).astype(jnp.int32)
    tm_d = _row_tile(n_s, 512)
    assert n_p % tm_d == 0 and tb % ZERO_ROWS == 0
    xs = _dispatch(jnp.concatenate([dest_p, dest_s]), h2_p, h2_s, pad_tab, rows, tm_d)
    ys = _experts(block_e, n_used, xs, expert_w_gu[l], expert_b_gu[l], expert_w_down[l], expert_b_down[l])
    y_p = _combine(dest_p, x2_p, gates_p, final_w, ys, tm_p)
    y_s = _combine(dest_s, x2_s, gates_s, final_w, ys, tm_s)

    hd = (FOX_HEADS, FOX_HEAD_DIM)
    seq_minor = lambda a: jnp.transpose(a.reshape(1, bp, *hd, sp), (0, 1, 4, 2, 3))
    return (y_p.reshape(bp, sp, d), y_s.reshape(bs, ts, d),
            seq_minor(pp["k"]), seq_minor(pp["v"]), logf_p[None],
            state_p[None], conv_p[None],
            ps["k"].reshape(1, bs, ts, *hd), ps["v"].reshape(1, bs, ts, *hd), logf_s[None],
            state_s[None], conv_s[None])
```
